```python
import math
import jax, jax.numpy as jnp
from jax import lax
import numpy as np

D_MODEL = 1024
BATCH = 1
SEQ = 16384
DEPTH = 4

GRID_W = 64
CTX_LEN = 256
CHUNK = 64
N_BRANCH = 3
BRANCH_W = D_MODEL // 2
NORM_EPS = 1e-6
RW_HEAD = 64
RW_HEADS = BRANCH_W // RW_HEAD
RW_DECAY_LORA = 64
RW_A_LORA = 64
RW_G_LORA = 128
RW_GN_EPS = 64e-5
GLA_HEADS = 4
GLA_DK = 64
GLA_DV = BRANCH_W // GLA_HEADS
GLA_GATE_LORA = 16
GLA_GATE_NORM = 16.0
GDN_HEADS = 4
GDN_HEAD = BRANCH_W // GDN_HEADS
GDN_CONV = 5
MLP_HIDDEN = 4 * D_MODEL
RW_COLS = 3 * BRANCH_W + 2 * RW_DECAY_LORA + 2 * RW_A_LORA + RW_G_LORA
GLA_COLS = 2 * GLA_HEADS * GLA_DK + 2 * BRANCH_W + 2 * GLA_GATE_LORA
GDN_COLS = 4 * BRANCH_W + 4 * GDN_HEADS
GATE_COLS = N_BRANCH * D_MODEL
IN_COLS = RW_COLS + GLA_COLS + GDN_COLS + GATE_COLS

kernel_name = 'hybrid_rwkv7_gla_gdn_diffusion_block'


def _split(t, sizes):
    return jnp.split(t, np.cumsum(sizes)[:-1].tolist(), axis=-1)


def rmsnorm(t, g):
    tf = t.astype(jnp.float32)
    tf = tf * lax.rsqrt(jnp.mean(tf * tf, axis=-1, keepdims=True) + NORM_EPS)
    return (tf * g.astype(jnp.float32)).astype(t.dtype)


def l2norm(t):
    tf = t.astype(jnp.float32)
    return tf * lax.rsqrt(jnp.sum(tf * tf, axis=-1, keepdims=True) + 1e-12)


def seg_flip(t, n_ctx):
    return jnp.concatenate([jnp.flip(t[:, :n_ctx], 1), jnp.flip(t[:, n_ctx:], 1)], axis=1)


def dir_stack(t_fwd, t_bwd, n_ctx):
    return jnp.stack([t_fwd, seg_flip(t_bwd, n_ctx)], axis=0)


def dir_pair(t, n_ctx):
    return dir_stack(t[:, :, 0], t[:, :, 1], n_ctx)


def dir_merge(o, n_ctx):
    return o[0] + seg_flip(o[1], n_ctx)


def _centred_dwconv(t, w):
    k = w.shape[0]
    pad = k // 2
    n = t.shape[1]
    tp = jnp.pad(t, ((0, 0), (pad, pad), (0, 0)))
    out = tp[:, 0:n] * w[0]
    for i in range(1, k):
        out = out + tp[:, i:i + n] * w[i]
    return out


def seg_dwconv(t, w, n_ctx):
    return jnp.concatenate([_centred_dwconv(t[:, :n_ctx], w), _centred_dwconv(t[:, n_ctx:], w)], axis=1)


def to_colmajor(t, n_ctx, rows):
    lat = t[:, n_ctx:]
    b, n = lat.shape[:2]
    tail = lat.shape[2:]
    lat = lat.reshape((b, rows, GRID_W) + tail).swapaxes(1, 2).reshape((b, n) + tail)
    return jnp.concatenate([t[:, :n_ctx], lat], axis=1)


def from_colmajor(t, n_ctx, rows):
    lat = t[:, n_ctx:]
    b, n = lat.shape[:2]
    tail = lat.shape[2:]
    lat = lat.reshape((b, GRID_W, rows) + tail).swapaxes(1, 2).reshape((b, n) + tail)
    return jnp.concatenate([t[:, :n_ctx], lat], axis=1)


def seg_modulate(t, n_ctx, shift_c, scale_c, shift_l, scale_l):
    tc = t[:, :n_ctx] * (1 + scale_c) + shift_c
    tl = t[:, n_ctx:] * (1 + scale_l[:, None]) + shift_l[:, None]
    return jnp.concatenate([tc, tl], axis=1)


def seg_gate(t, n_ctx, gate_c, gate_l):
    return jnp.concatenate([t[:, :n_ctx] * gate_c, t[:, n_ctx:] * gate_l[:, None]], axis=1)


def rwkv7_scan(r, w, k, v, kk, b):
    def step(s, inp):
        r_t, w_t, k_t, v_t, kk_t, b_t = inp
        sa = jnp.einsum('zbhvk,zbhk->zbhv', s, kk_t)
        s = s * w_t[..., None, :] - sa[..., :, None] * b_t[..., None, :] + v_t[..., :, None] * k_t[..., None, :]
        return s, jnp.einsum('zbhvk,zbhk->zbhv', s, r_t)
    nd, bsz, _, h, n = r.shape
    s0 = jnp.zeros((nd, bsz, h, n, n), jnp.float32)
    xs = tuple(jnp.moveaxis(t, 2, 0) for t in (r, w, k, v, kk, b))
    _, y = lax.scan(step, s0, xs)
    return jnp.moveaxis(y, 0, 2)


def gla_chunked(q, k, v, log_a):
    nd, bsz, L, h, dk = q.shape
    dv = v.shape[-1]
    n = L // CHUNK
    cs = lambda t: t.reshape(nd, bsz, n, CHUNK, h, t.shape[-1])
    q, k, v, log_a = cs(q), cs(k), cs(v), cs(log_a)
    bcum = jnp.cumsum(log_a, axis=3)
    b_end = bcum[:, :, :, -1:]
    q_dec = q * jnp.exp(bcum)
    k_inv = k * jnp.exp(-bcum)
    k_end = k * jnp.exp(b_end - bcum)
    causal = jnp.tril(jnp.ones((CHUNK, CHUNK), jnp.float32))
    att = jnp.einsum('zbnthk,zbnshk->zbnhts', q_dec, k_inv) * causal
    o_intra = jnp.einsum('zbnhts,zbnshv->zbnthv', att, v)
    decay_end = jnp.exp(b_end[:, :, :, 0])

    def step(s, inp):
        q_c, k_c, v_c, dec = inp
        o = jnp.einsum('zbthk,zbhkv->zbthv', q_c, s)
        s = s * dec[..., None] + jnp.einsum('zbshk,zbshv->zbhkv', k_c, v_c)
        return s, o
    s0 = jnp.zeros((nd, bsz, h, dk, dv), jnp.float32)
    xs = tuple(jnp.moveaxis(t, 2, 0) for t in (q_dec, k_end, v, decay_end))
    _, o_inter = lax.scan(step, s0, xs)
    return (o_intra + jnp.moveaxis(o_inter, 0, 2)).reshape(nd, bsz, L, h, dv)


def gated_delta_chunked(q, k, v, g, beta):
    nd, bsz, L, h, dk = q.shape
    dv = v.shape[-1]
    n = L // CHUNK
    heads_first = lambda t: jnp.moveaxis(t.reshape(nd, bsz, n, CHUNK, h, -1), 4, 3)
    q, k, v = heads_first(q), heads_first(k), heads_first(v)
    g = jnp.moveaxis(g.reshape(nd, bsz, n, CHUNK, h), 4, 3)
    beta = jnp.moveaxis(beta.reshape(nd, bsz, n, CHUNK, h), 4, 3)
    gam = jnp.cumsum(g, axis=-1)
    incl = jnp.tril(jnp.ones((CHUNK, CHUNK), bool))
    strict = jnp.tril(jnp.ones((CHUNK, CHUNK), bool), -1)
    diff = gam[..., :, None] - gam[..., None, :]
    decay = jnp.where(incl, jnp.exp(jnp.where(incl, diff, 0.0)), 0.0)
    kk = jnp.einsum('zbnhtk,zbnhsk->zbnhts', k, k)
    a_mat = jnp.eye(CHUNK, dtype=jnp.float32) + jnp.where(strict, beta[..., :, None] * kk * decay, 0.0)
    u = lax.linalg.triangular_solve(a_mat, beta[..., None] * v, left_side=True, lower=True, unit_diagonal=True)
    wk = lax.linalg.triangular_solve(a_mat, (beta * jnp.exp(gam))[..., None] * k, left_side=True, lower=True, unit_diagonal=True)
    qk = jnp.einsum('zbnhtk,zbnhsk->zbnhts', q, k) * decay
    q_dec = q * jnp.exp(gam)[..., None]
    k_end = k * jnp.exp(gam[..., -1:] - gam)[..., None]
    dec_end = jnp.exp(gam[..., -1])

    def step(s, inp):
        u_c, wk_c, qk_c, qd_c, ke_c, de_c = inp
        w_c = u_c - jnp.einsum('zbhsk,zbhkv->zbhsv', wk_c, s)
        o = jnp.einsum('zbhtk,zbhkv->zbhtv', qd_c, s) + jnp.einsum('zbhts,zbhsv->zbhtv', qk_c, w_c)
        s = s * de_c[..., None, None] + jnp.einsum('zbhsk,zbhsv->zbhkv', ke_c, w_c)
        return s, o
    s0 = jnp.zeros((nd, bsz, h, dk, dv), jnp.float32)
    xs = tuple(jnp.moveaxis(t, 2, 0) for t in (u, wk, qk, q_dec, k_end, dec_end))
    _, o = lax.scan(step, s0, xs)
    o = jnp.moveaxis(o, 0, 2)
    return jnp.moveaxis(o, 3, 4).reshape(nd, bsz, L, h, dv)


def rwkv7_branch(p, n_ctx, w0, w2, a0, a2, g2, k_k, k_a, r_k, ln_w, ln_b):
    bsz, L, _ = p.shape
    H, N = RW_HEADS, RW_HEAD
    f32 = jnp.float32
    r, k, v, wl, al, gl = _split(p, [BRANCH_W] * 3 + [2 * RW_DECAY_LORA, 2 * RW_A_LORA, RW_G_LORA])
    wl = wl.reshape(bsz, L, 2, RW_DECAY_LORA)
    al = al.reshape(bsz, L, 2, RW_A_LORA)
    w = -jax.nn.softplus(-(w0 + jnp.einsum('blze,zec->blzc', jnp.tanh(wl), w2))) - 0.5
    decay = jnp.exp(-jnp.exp(w.astype(f32)))
    a = jax.nn.sigmoid(a0 + jnp.einsum('blze,zec->blzc', al, a2)).astype(f32)
    g = jax.nn.sigmoid(gl) @ g2
    kk = l2norm((k * k_k).reshape(bsz, L, H, N))
    k_dir = (k[:, :, None] * (1 + (a - 1) * k_a)).reshape(bsz, L, 2, H, N)
    b_dir = kk[:, :, None] * a.reshape(bsz, L, 2, H, N)
    r_h = r.reshape(bsz, L, H, N).astype(f32)
    v_h = v.reshape(bsz, L, H, N).astype(f32)
    y = rwkv7_scan(dir_stack(r_h, r_h, n_ctx), dir_pair(decay.reshape(bsz, L, 2, H, N), n_ctx),
                   dir_pair(k_dir, n_ctx), dir_stack(v_h, v_h, n_ctx), dir_stack(kk, kk, n_ctx),
                   dir_pair(b_dir, n_ctx))
    y = dir_merge(y, n_ctx)
    mu = jnp.mean(y, axis=-1, keepdims=True)
    var = jnp.mean(jnp.square(y - mu), axis=-1, keepdims=True)
    y = (y - mu) * lax.rsqrt(var + RW_GN_EPS) * ln_w.reshape(H, N) + ln_b.reshape(H, N)
    bonus = jnp.einsum('blhn,blzhn,hn->blh', r_h, k_dir, r_k)[..., None] * v_h
    y = (y + bonus).reshape(bsz, L, BRANCH_W)
    return (y * g.astype(f32)).astype(p.dtype)


def gla_branch(p, n_ctx, rows, a2, ab, norm_g):
    bsz, L, _ = p.shape
    H, K, V = GLA_HEADS, GLA_DK, GLA_DV
    f32 = jnp.float32
    q, k, v, og, al = _split(p, [H * K, H * K, H * V, H * V, 2 * GLA_GATE_LORA])
    z = jnp.einsum('blze,zek->blzk', al.reshape(bsz, L, 2, GLA_GATE_LORA), a2) + ab
    log_a = (jax.nn.log_sigmoid(z.astype(f32)) / GLA_GATE_NORM).reshape(bsz, L, 2, H, K)
    q = q.reshape(bsz, L, H, K).astype(f32) * K ** -0.5
    k = k.reshape(bsz, L, H, K).astype(f32)
    v = v.reshape(bsz, L, H, V).astype(f32)
    q, k, v, log_a = (to_colmajor(t, n_ctx, rows) for t in (q, k, v, log_a))
    o = gla_chunked(dir_stack(q, q, n_ctx), dir_stack(k, k, n_ctx), dir_stack(v, v, n_ctx), dir_pair(log_a, n_ctx))
    o = from_colmajor(dir_merge(o, n_ctx), n_ctx, rows)
    o = rmsnorm(o, norm_g) * jax.nn.silu(og.reshape(bsz, L, H, V).astype(f32))
    return o.reshape(bsz, L, BRANCH_W).astype(p.dtype)


def gdn_branch(p, n_ctx, conv_w, a_log, dt_bias, norm_g):
    bsz, L, _ = p.shape
    H, K = GDN_HEADS, GDN_HEAD
    f32 = jnp.float32
    qkv, zg, a, b = _split(p, [3 * BRANCH_W, BRANCH_W, 2 * H, 2 * H])
    qkv = jax.nn.silu(seg_dwconv(qkv, conv_w, n_ctx))
    q, k, v = _split(qkv, [BRANCH_W] * 3)
    q = l2norm(q.reshape(bsz, L, H, K)) * K ** -0.5
    k = l2norm(k.reshape(bsz, L, H, K))
    v = v.reshape(bsz, L, H, K).astype(f32)
    g = -jnp.exp(a_log.astype(f32)) * jax.nn.softplus((a.reshape(bsz, L, 2, H) + dt_bias).astype(f32))
    beta = jax.nn.sigmoid(b.reshape(bsz, L, 2, H).astype(f32))
    o = gated_delta_chunked(dir_stack(q, q, n_ctx), dir_stack(k, k, n_ctx), dir_stack(v, v, n_ctx),
                            dir_pair(g, n_ctx), dir_pair(beta, n_ctx))
    o = dir_merge(o, n_ctx)
    o = rmsnorm(o, norm_g) * jax.nn.silu(zg.reshape(bsz, L, H, K).astype(f32))
    return o.reshape(bsz, L, BRANCH_W).astype(p.dtype)


def sq_relu_mlp(t, w1, w2):
    return jnp.square(jax.nn.relu(t @ w1)) @ w2


def setup_inputs(seed: int = 0) -> dict:
    key = jax.random.key(seed)
    keys = list(jax.random.split(key, 40))

    def nrm(shape, scale):
        return jax.random.normal(keys.pop(), shape, jnp.float32) * scale

    def uni(shape, lo, hi):
        return jax.random.uniform(keys.pop(), shape, jnp.float32, lo, hi)

    D, NL = D_MODEL, DEPTH
    dt = jnp.exp(uni((NL, 2, GDN_HEADS), math.log(1e-3), math.log(1e-1)))
    return {
        'x': nrm((BATCH, SEQ, D), 1.0),
        'c': nrm((BATCH, D), 1.0),
        'ctx': nrm((BATCH, CTX_LEN, D), 1.0),
        'c_ctx': nrm((D,), 1.0),
        'w_mod': nrm((NL, D, 6 * D), 0.02),
        'b_mod': nrm((NL, 6 * D), 0.01),
        'norm1_g': 1.0 + nrm((NL, D), 0.02),
        'w_in': nrm((NL, D, IN_COLS), D ** -0.5),
        'rw_mu': uni((NL, 2, RW_COLS), 0.0, 0.5),
        'rw_w0': uni((NL, 2, BRANCH_W), -6.0, -1.0),
        'rw_w2': nrm((NL, 2, RW_DECAY_LORA, BRANCH_W), 0.1),
        'rw_a0': nrm((NL, 2, BRANCH_W), 0.1),
        'rw_a2': nrm((NL, 2, RW_A_LORA, BRANCH_W), RW_A_LORA ** -0.5),
        'rw_g2': nrm((NL, RW_G_LORA, BRANCH_W), RW_G_LORA ** -0.5),
        'rw_kk': 0.85 + nrm((NL, BRANCH_W), 0.02),
        'rw_ka': 1.0 + nrm((NL, BRANCH_W), 0.02),
        'rw_rk': nrm((NL, RW_HEADS, RW_HEAD), 0.1),
        'rw_ln_w': 1.0 + nrm((NL, BRANCH_W), 0.02),
        'rw_ln_b': nrm((NL, BRANCH_W), 0.01),
        'gla_a2': nrm((NL, 2, GLA_GATE_LORA, GLA_HEADS * GLA_DK), GLA_GATE_LORA ** -0.5),
        'gla_ab': uni((NL, 2, GLA_HEADS * GLA_DK), 0.5, 3.0),
        'gla_norm_g': 1.0 + nrm((NL, GLA_DV), 0.02),
        'gdn_conv': nrm((NL, GDN_CONV, 3 * BRANCH_W), GDN_CONV ** -0.5),
        'gdn_a_log': jnp.log(uni((NL, 2, GDN_HEADS), 1.0, 16.0)),
        'gdn_dt_bias': dt + jnp.log(-jnp.expm1(-dt)),
        'gdn_norm_g': 1.0 + nrm((NL, GDN_HEAD), 0.02),
        'w_branch': nrm((NL, N_BRANCH, BRANCH_W, D), BRANCH_W ** -0.5),
        'w_out': nrm((NL, D, D), D ** -0.5),
        'norm2_g': 1.0 + nrm((NL, D), 0.02),
        'w_mlp1': nrm((NL, D, MLP_HIDDEN), D ** -0.5),
        'w_mlp2': nrm((NL, MLP_HIDDEN, D), MLP_HIDDEN ** -0.5),
        'final_g': 1.0 + nrm((D,), 0.02),
    }


def reference(x, c, ctx, c_ctx, w_mod, b_mod, norm1_g, w_in, rw_mu, rw_w0, rw_w2, rw_a0, rw_a2, rw_g2,
              rw_kk, rw_ka, rw_rk, rw_ln_w, rw_ln_b, gla_a2, gla_ab, gla_norm_g, gdn_conv, gdn_a_log,
              gdn_dt_bias, gdn_norm_g, w_branch, w_out, norm2_g, w_mlp1, w_mlp2, final_g):
    bsz, n_lat, dm = x.shape
    rows = n_lat // GRID_W
    n_ctx = ctx.shape[1]
    h = jnp.concatenate([ctx, x], axis=1)
    for l in range(DEPTH):
        m_lat = jnp.split(jax.nn.silu(c) @ w_mod[l] + b_mod[l], 6, axis=-1)
        m_ctx = jnp.split(jax.nn.silu(c_ctx) @ w_mod[l] + b_mod[l], 6, axis=-1)
        hn = seg_modulate(rmsnorm(h, norm1_g[l]), n_ctx, m_ctx[0], m_ctx[1], m_lat[0], m_lat[1])
        p = hn @ w_in[l]
        p_rw, p_gla, p_gdn, p_gate = _split(p, [RW_COLS, GLA_COLS, GDN_COLS, GATE_COLS])
        mu = rw_mu[l]
        p_rw = seg_dwconv(p_rw, jnp.stack([mu[0], 1 - mu[0] - mu[1], mu[1]]), n_ctx)
        y_rw = rwkv7_branch(p_rw, n_ctx, rw_w0[l], rw_w2[l], rw_a0[l], rw_a2[l], rw_g2[l], rw_kk[l],
                            rw_ka[l], rw_rk[l], rw_ln_w[l], rw_ln_b[l])
        y_gla = gla_branch(p_gla, n_ctx, rows, gla_a2[l], gla_ab[l], gla_norm_g[l])
        y_gdn = gdn_branch(p_gdn, n_ctx, gdn_conv[l], gdn_a_log[l], gdn_dt_bias[l], gdn_norm_g[l])
        branches = jnp.stack([y_rw, y_gla, y_gdn], axis=2)
        gates = jax.nn.sigmoid(p_gate).reshape(bsz, -1, N_BRANCH, dm)
        merged = jnp.sum(jnp.einsum('blgc,gcd->blgd', branches, w_branch[l]) * gates, axis=2)
        mix = merged @ w_out[l]
        if l < DEPTH - 1:
            h = h + seg_gate(mix, n_ctx, m_ctx[2], m_lat[2])
            hn = seg_modulate(rmsnorm(h, norm2_g[l]), n_ctx, m_ctx[3], m_ctx[4], m_lat[3], m_lat[4])
            h = h + seg_gate(sq_relu_mlp(hn, w_mlp1[l], w_mlp2[l]), n_ctx, m_ctx[5], m_lat[5])
        else:
            h = h[:, n_ctx:] + mix[:, n_ctx:] * m_lat[2][:, None]
            hn = rmsnorm(h, norm2_g[l]) * (1 + m_lat[4][:, None]) + m_lat[3][:, None]
            h = h + sq_relu_mlp(hn, w_mlp1[l], w_mlp2[l]) * m_lat[5][:, None]
    return rmsnorm(h, final_g)
```

```python
import functools

import jax
import jax.numpy as jnp
from jax import lax
from jax.experimental import pallas as pl
from jax.experimental.pallas import tpu as pltpu

F32 = jnp.float32
BF16 = jnp.bfloat16

D_MODEL = 1024
GRID_W = 64
CHUNK = 64
BRANCH_W = D_MODEL // 2
NORM_EPS = 1e-6
RW_HEAD = 64
RW_HEADS = BRANCH_W // RW_HEAD
RW_PAIRS = RW_HEADS // 2
RW_DECAY_LORA = 64
RW_A_LORA = 64
RW_G_LORA = 128
RW_GN_EPS = 64e-5
GLA_HEADS = 4
GLA_DK = 64
GLA_DV = BRANCH_W // GLA_HEADS
GLA_GATE_LORA = 16
GLA_GATE_NORM = 16.0
GDN_HEADS = 4
GDN_HEAD = BRANCH_W // GDN_HEADS
GDN_CONV = 5
MLP_HIDDEN = 4 * D_MODEL
RW_COLS = 3 * BRANCH_W + 2 * RW_DECAY_LORA + 2 * RW_A_LORA + RW_G_LORA
GLA_COLS = 2 * GLA_HEADS * GLA_DK + 2 * BRANCH_W + 2 * GLA_GATE_LORA
GDN_COLS = 4 * BRANCH_W + 4 * GDN_HEADS

LANES = 128
SUBLANES = 8
GLA_COLS_PAD = -(-GLA_COLS // LANES) * LANES
GDN_COLS_PAD = -(-GDN_COLS // LANES) * LANES
VMEM_LIMIT_BYTES = 48 * 1024 * 1024

NN = (((1,), (0,)), ((), ()))
NT = (((1,), (1,)), ((), ()))
TN = (((0,), (0,)), ((), ()))


def _dg(a, b, dims=NN):
    return lax.dot_general(a, b, dims, preferred_element_type=F32)


def _dot1(a, b, dims=NN):
    return _dg(a.astype(BF16), b.astype(BF16), dims)


def _split2(a):
    hi = a.astype(BF16)
    lo = (a - hi.astype(F32)).astype(BF16)
    return hi, lo


def _dot3(a, b, dims=NN):
    ah, al = _split2(a)
    bh, bl = _split2(b)
    return _dg(ah, bh, dims) + (_dg(ah, bl, dims) + _dg(al, bh, dims))


def _dotc(c_bf16, x, dims=NN):
    h1 = x.astype(BF16)
    r1 = x - h1.astype(F32)
    h2 = r1.astype(BF16)
    h3 = (r1 - h2.astype(F32)).astype(BF16)
    return _dg(c_bf16, h1, dims) + (_dg(c_bf16, h2, dims) + _dg(c_bf16, h3, dims))


def _dotxc(x, c_bf16, dims=NN):
    h1 = x.astype(BF16)
    r1 = x - h1.astype(F32)
    h2 = r1.astype(BF16)
    h3 = (r1 - h2.astype(F32)).astype(BF16)
    return _dg(h1, c_bf16, dims) + (_dg(h2, c_bf16, dims) + _dg(h3, c_bf16, dims))


def _iota(shape, axis):
    return lax.broadcasted_iota(jnp.int32, shape, axis)


def _order_masks(n, rev):
    row = _iota((n, n), 0)
    col = _iota((n, n), 1)
    if rev:
        return col >= row, col > row
    return col <= row, col < row


def _tri_inv(a, dot):
    n = a.shape[0]
    eye = (_iota((n, n), 0) == _iota((n, n), 1)).astype(F32)
    t = eye - a
    p = a
    k = 2
    while k < n:
        p = dot(p, p)
        t = t + dot(t, p)
        k *= 2
    return t


def _sigmoid(x):
    return 1.0 / (1.0 + jnp.exp(-x))


def _softplus(x):
    return jnp.maximum(x, 0.0) + jnp.log(1.0 + jnp.exp(-jnp.abs(x)))


def _silu(x):
    return x * _sigmoid(x)


def _params(sem):
    return pltpu.CompilerParams(dimension_semantics=sem, vmem_limit_bytes=VMEM_LIMIT_BYTES)


def _mod_kernel(cc_ref, w_ref, b_ref, o_ref):
    s = _silu(cc_ref[...])
    o_ref[...] = _dot3(s, w_ref[...]) + b_ref[...]


def _modulation(cc, w_mod, b_mod):
    depth = w_mod.shape[0]
    ncol = w_mod.shape[2] // D_MODEL
    return pl.pallas_call(
        _mod_kernel,
        grid=(depth, ncol),
        in_specs=[
            pl.BlockSpec((SUBLANES, D_MODEL), lambda l, j: (0, 0)),
            pl.BlockSpec((None, D_MODEL, D_MODEL), lambda l, j: (l, 0, j)),
            pl.BlockSpec((None, 1, D_MODEL), lambda l, j: (l, 0, j)),
        ],
        out_specs=pl.BlockSpec((None, SUBLANES, D_MODEL), lambda l, j: (l, 0, j)),
        out_shape=jax.ShapeDtypeStruct((depth, SUBLANES, w_mod.shape[2]), F32),
        compiler_params=_params(("parallel", "parallel")),
        name="modulation",
    )(cc, w_mod, b_mod.reshape(depth, 1, -1))


def _seg_vec(mod_ref, idx, row0, nrows, n_ctx):
    lo = idx * D_MODEL
    ctx = mod_ref[0:1, lo:lo + D_MODEL]
    lat = mod_ref[1:2, lo:lo + D_MODEL]
    is_ctx = (_iota((nrows, 1), 0) + row0) < n_ctx
    return jnp.where(is_ctx, ctx, lat)


def _rmsnorm_rows(x, g):
    ms = jnp.mean(x * x, axis=-1, keepdims=True)
    return x * lax.rsqrt(ms + NORM_EPS) * g


def _norm_kernel(x_ref, g_ref, mod_ref, o_ref, *, n_ctx, tb, shift_idx, scale_idx):
    row0 = pl.program_id(0) * tb
    xn = _rmsnorm_rows(x_ref[...], g_ref[...])
    if scale_idx is not None:
        xn = xn * (1.0 + _seg_vec(mod_ref, scale_idx, row0, tb, n_ctx)) \
            + _seg_vec(mod_ref, shift_idx, row0, tb, n_ctx)
    o_ref[...] = xn.astype(o_ref.dtype)


def _norm(x, g, mod, n_ctx, tb, out_dtype, shift_idx=0, scale_idx=1):
    n = x.shape[0]
    return pl.pallas_call(
        functools.partial(_norm_kernel, n_ctx=n_ctx, tb=tb, shift_idx=shift_idx, scale_idx=scale_idx),
        grid=(n // tb,),
        in_specs=[
            pl.BlockSpec((tb, D_MODEL), lambda i: (i, 0)),
            pl.BlockSpec((1, D_MODEL), lambda i: (0, 0)),
            pl.BlockSpec(mod.shape, lambda i: (0, 0)),
        ],
        out_specs=pl.BlockSpec((tb, D_MODEL), lambda i: (i, 0)),
        out_shape=jax.ShapeDtypeStruct((n, D_MODEL), out_dtype),
        compiler_params=_params(("parallel",)),
        name="adaln_norm",
    )(x, g, mod)


def _mm_kernel(a_ref, w_ref, o_ref):
    o_ref[...] = _dg(a_ref[...], w_ref[...])


def _project(a, w, tm, name):
    n, k = a.shape
    cols = w.shape[1]
    return pl.pallas_call(
        _mm_kernel,
        grid=(n // tm,),
        in_specs=[pl.BlockSpec((tm, k), lambda i: (i, 0)),
                  pl.BlockSpec((k, cols), lambda i: (0, 0))],
        out_specs=pl.BlockSpec((tm, cols), lambda i: (i, 0)),
        out_shape=jax.ShapeDtypeStruct((n, cols), F32),
        compiler_params=_params(("parallel",)),
        name=name,
    )(a, w)


def _fill_ext(ext_ref, cur_ref, prev_ref, next_ref, has_prev, has_next, tb):
    zeros = jnp.zeros(prev_ref.shape, F32)
    ext_ref[0:SUBLANES, :] = jnp.where(has_prev, prev_ref[...], zeros)
    ext_ref[SUBLANES:SUBLANES + tb, :] = cur_ref[...]
    ext_ref[SUBLANES + tb:SUBLANES + tb + SUBLANES, :] = jnp.where(has_next, next_ref[...], zeros)


def _halo_specs(tb, cols, col_block=0):
    r = tb // SUBLANES

    def prev_map(i):
        return (jnp.maximum(i * r - 1, 0), col_block)

    def next_map(i, nblk):
        return (jnp.minimum((i + 1) * r, nblk * r - 1), col_block)

    return r, prev_map, next_map


def _rw_prep_kernel(cur_ref, prev_ref, next_ref, mu_ref, w0_ref, w2_ref, a0_ref, a2_ref, g2_ref,
                    kka_ref, bd_ref,
                    r_ref, v_ref, kk_ref, lw_ref, kd_ref, bdir_ref, g_ref, bonus_ref,
                    ext_ref, *, tb, ctx_blocks, nblk):
    i = pl.program_id(0)
    seg_first = jnp.logical_or(i == 0, i == ctx_blocks)
    seg_last = jnp.logical_or(i == ctx_blocks - 1, i == nblk - 1)
    _fill_ext(ext_ref, cur_ref, prev_ref, next_ref, jnp.logical_not(seg_first),
              jnp.logical_not(seg_last), tb)
    xs = (mu_ref[0:1, :] * ext_ref[pl.ds(SUBLANES - 1, tb), :]
          + mu_ref[1:2, :] * ext_ref[pl.ds(SUBLANES, tb), :]
          + mu_ref[2:3, :] * ext_ref[pl.ds(SUBLANES + 1, tb), :])
    bw = BRANCH_W
    r = xs[:, 0:bw]
    k = xs[:, bw:2 * bw]
    v = xs[:, 2 * bw:3 * bw]
    wl = xs[:, 3 * bw:3 * bw + LANES]
    al = xs[:, 3 * bw + LANES:3 * bw + 2 * LANES]
    gl = xs[:, 3 * bw + 2 * LANES:3 * bw + 3 * LANES]
    bd = bd_ref[...]
    k_k = kka_ref[0:1, :]
    k_a = kka_ref[1:2, :]
    r_k = kka_ref[2:3, :]

    kk0 = k * k_k
    kk = kk0 * lax.rsqrt(_dotxc(kk0 * kk0, bd) + 1e-12)
    g = _dot3(_sigmoid(gl), g2_ref[...])
    twl = jnp.tanh(wl)
    ksum = jnp.zeros_like(k)
    for z in range(2):
        wlog = w0_ref[z:z + 1, :] + _dot3(twl, w2_ref[z])
        w = -_softplus(-wlog) - 0.5
        lw = -jnp.exp(w)
        a = _sigmoid(a0_ref[z:z + 1, :] + _dot3(al, a2_ref[z]))
        kd = k * (1.0 + (a - 1.0) * k_a)
        bdir = kk * a
        ksum = ksum + kd
        for p in range(RW_PAIRS):
            sl = slice(p * LANES, (p + 1) * LANES)
            lw_ref[z, p] = lw[:, sl]
            kd_ref[z, p] = kd[:, sl]
            bdir_ref[z, p] = bdir[:, sl]
    bonus = _dotxc(r * ksum * r_k, bd) * v
    for p in range(RW_PAIRS):
        sl = slice(p * LANES, (p + 1) * LANES)
        r_ref[p] = r[:, sl]
        v_ref[p] = v[:, sl]
        kk_ref[p] = kk[:, sl]
        g_ref[p] = g[:, sl]
        bonus_ref[p] = bonus[:, sl]


def _rw_prep(p_rw, wts, n_ctx, tb):
    n = p_rw.shape[0]
    nblk = n // tb
    r, prev_map, next_map = _halo_specs(tb, RW_COLS)
    full = lambda a: pl.BlockSpec(a.shape, lambda i: (0,) * a.ndim)
    pair_spec = pl.BlockSpec((RW_PAIRS, tb, LANES), lambda i: (0, i, 0))
    dir_spec = pl.BlockSpec((2, RW_PAIRS, tb, LANES), lambda i: (0, 0, i, 0))
    pair_shape = jax.ShapeDtypeStruct((RW_PAIRS, n, LANES), F32)
    dir_shape = jax.ShapeDtypeStruct((2, RW_PAIRS, n, LANES), F32)
    params = [wts["mu3"], wts["w0"], wts["w2p"], wts["a0"], wts["a2p"], wts["g2"], wts["kka"], wts["bd64"]]
    return pl.pallas_call(
        functools.partial(_rw_prep_kernel, tb=tb, ctx_blocks=n_ctx // tb, nblk=nblk),
        grid=(nblk,),
        in_specs=[pl.BlockSpec((tb, RW_COLS), lambda i: (i, 0)),
                  pl.BlockSpec((SUBLANES, RW_COLS), prev_map),
                  pl.BlockSpec((SUBLANES, RW_COLS), functools.partial(next_map, nblk=nblk))]
                 + [full(a) for a in params],
        out_specs=[pair_spec, pair_spec, pair_spec, dir_spec, dir_spec, dir_spec, pair_spec, pair_spec],
        out_shape=[pair_shape, pair_shape, pair_shape, dir_shape, dir_shape, dir_shape, pair_shape, pair_shape],
        scratch_shapes=[pltpu.VMEM((tb + 2 * SUBLANES, RW_COLS), F32)],
        compiler_params=_params(("parallel",)),
        name="rwkv_prep",
    )(p_rw, p_rw, p_rw, *params)


def _rw_chunk_dir(r, v, kk, lw, kd, bdir, m_ref, rev):
    n = CHUNK
    incl, strict = _order_masks(n, rev)
    cum = _dotc(incl.astype(BF16), lw)
    cum_end = cum[0:1, :] if rev else cum[n - 1:n, :]
    e_in = jnp.exp(cum)
    e_ex = jnp.exp(cum - lw)
    e_inv = jnp.exp(-cum)
    e_end = jnp.exp(cum_end - cum)
    g_end = jnp.exp(cum_end)
    kkd = kk * e_ex
    kinv = kd * e_inv
    binv = bdir * e_inv
    rd = r * e_in
    kend = kd * e_end
    bend = bdir * e_end
    head_a = _iota((n, LANES), 1) < RW_HEAD
    parts = []
    for h in range(2):
        mh = head_a if h == 0 else jnp.logical_not(head_a)
        kkd_h = jnp.where(mh, kkd, 0.0)
        rd_h = jnp.where(mh, rd, 0.0)
        a_kb = jnp.where(strict, _dot1(kkd_h, binv, NT), 0.0)
        a_kk = jnp.where(strict, _dot1(kkd_h, kinv, NT), 0.0)
        a_rb = jnp.where(incl, _dot1(rd_h, binv, NT), 0.0)
        a_rk = jnp.where(incl, _dot1(rd_h, kinv, NT), 0.0)
        t = _tri_inv(a_kb, _dot3)
        u0 = _dot3(t, _dot1(a_kk, v))
        w = _dot3(t, kkd)
        y0 = _dot1(a_rk, v) - _dot1(a_rb, u0)
        qp = rd - _dot1(a_rb, w)
        parts.append((u0, w, y0, qp))
    u0, w, y0, qp = (jnp.where(head_a, pa, pb) for pa, pb in zip(parts[0], parts[1]))
    m = m_ref[...]
    y = y0 + _dot3(qp, m)
    row = _iota((LANES, LANES), 0)
    col = _iota((LANES, LANES), 1)
    same_head = (row < RW_HEAD) == (col < RW_HEAD)
    p_mat = jnp.where(row == col, jnp.broadcast_to(g_end, (LANES, LANES)), 0.0) - _dot1(bend, w, TN)
    z0 = _dot1(kend, v, TN) - _dot1(bend, u0, TN)
    m_ref[...] = jnp.where(same_head, _dot3(jnp.where(same_head, p_mat, 0.0), m) + z0, 0.0)
    return y


def _rw_chunk_kernel(rf, vf, kkf, lwf, kdf, bf, rb, vb, kkb, lwb, kdb, bb, yf_ref, yb_ref, m_ref):
    @pl.when(pl.program_id(1) == 0)
    def _():
        m_ref[...] = jnp.zeros(m_ref.shape, F32)

    yf_ref[...] = _rw_chunk_dir(rf[...], vf[...], kkf[...], lwf[...], kdf[...], bf[...], m_ref.at[0], False)
    yb_ref[...] = _rw_chunk_dir(rb[...], vb[...], kkb[...], lwb[...], kdb[...], bb[...], m_ref.at[1], True)


def _chunk_maps(n, n_ctx):
    n_chunks = n // CHUNK
    ncc = n_ctx // CHUNK

    def fwd(i):
        return i

    def bwd(i):
        return jnp.where(i < ncc, ncc - 1 - i, n_chunks + ncc - 1 - i)

    return n_chunks, fwd, bwd


def _rw_chunk(r, v, kk, lw, kd, bdir, n_ctx):
    n = r.shape[1]
    n_chunks, fwd, bwd = _chunk_maps(n, n_ctx)

    def shared(cm):
        return pl.BlockSpec((None, CHUNK, LANES), lambda p, i: (p, cm(i), 0))

    def per_dir(z, cm):
        return pl.BlockSpec((None, None, CHUNK, LANES), lambda p, i: (z, p, cm(i), 0))

    in_specs = [shared(fwd), shared(fwd), shared(fwd), per_dir(0, fwd), per_dir(0, fwd), per_dir(0, fwd),
                shared(bwd), shared(bwd), shared(bwd), per_dir(1, bwd), per_dir(1, bwd), per_dir(1, bwd)]
    shape = jax.ShapeDtypeStruct((RW_PAIRS, n, LANES), F32)
    return pl.pallas_call(
        _rw_chunk_kernel,
        grid=(RW_PAIRS, n_chunks),
        in_specs=in_specs,
        out_specs=[shared(fwd), shared(bwd)],
        out_shape=[shape, shape],
        scratch_shapes=[pltpu.VMEM((2, LANES, LANES), F32)],
        compiler_params=_params(("parallel", "arbitrary")),
        name="rwkv_chunk",
    )(r, v, kk, lw, kd, bdir, r, v, kk, lw, kd, bdir)


def _rw_post_kernel(yf_ref, yb_ref, bonus_ref, g_ref, ln_ref, bd_ref, o_ref):
    bd = bd_ref[...]
    inv_n = 1.0 / RW_HEAD
    for p in range(RW_PAIRS):
        sl = slice(p * LANES, (p + 1) * LANES)
        y = yf_ref[p] + yb_ref[p]
        mu = _dotxc(y, bd) * inv_n
        d = y - mu
        var = _dotxc(d * d, bd) * inv_n
        yn = d * lax.rsqrt(var + RW_GN_EPS) * ln_ref[0:1, sl] + ln_ref[1:2, sl]
        o_ref[:, sl] = (yn + bonus_ref[p]) * g_ref[p]


def _rw_post(yf, yb, bonus, g, ln, bd128, tb):
    n = yf.shape[1]
    pair_spec = pl.BlockSpec((RW_PAIRS, tb, LANES), lambda i: (0, i, 0))
    full = lambda a: pl.BlockSpec(a.shape, lambda i: (0,) * a.ndim)
    return pl.pallas_call(
        _rw_post_kernel,
        grid=(n // tb,),
        in_specs=[pair_spec, pair_spec, pair_spec, pair_spec, full(ln), full(bd128)],
        out_specs=pl.BlockSpec((tb, BRANCH_W), lambda i: (i, 0)),
        out_shape=jax.ShapeDtypeStruct((n, BRANCH_W), F32),
        compiler_params=_params(("parallel",)),
        name="rwkv_post",
    )(yf, yb, bonus, g, ln, bd128)


def _gla_chunk_dir(q, k, v_ref, al, a2_ref, ab_ref, s_ref, z, rev):
    n = CHUNK
    incl, _ = _order_masks(n, rev)
    tri = incl.astype(BF16)
    zlog = _dot3(al, a2_ref[z]) + ab_ref[z:z + 1, :]
    log_a = (jnp.minimum(zlog, 0.0) - jnp.log(1.0 + jnp.exp(-jnp.abs(zlog)))) * (1.0 / GLA_GATE_NORM)
    bcum = _dotc(tri, log_a)
    b_end = bcum[0:1, :] if rev else bcum[n - 1:n, :]
    q_dec = q * (GLA_DK ** -0.5) * jnp.exp(bcum)
    k_inv = k * jnp.exp(-bcum)
    k_end = k * jnp.exp(b_end - bcum)
    ones = jnp.ones((n, LANES), BF16)
    head_a = _iota((n, LANES), 1) < GLA_DK
    top = _iota((LANES, LANES), 0) < GLA_DK
    outs = []
    for p in range(GLA_HEADS // 2):
        sl = slice(p * LANES, (p + 1) * LANES)
        qd_p, ki_p, ke_p = q_dec[:, sl], k_inv[:, sl], k_end[:, sl]
        dec_col = jnp.exp(_dotxc(log_a[:, sl], ones, TN))
        s = s_ref[p]
        upd = []
        for h in range(2):
            mh = head_a if h == 0 else jnp.logical_not(head_a)
            vh = v_ref[:, (2 * p + h) * GLA_DV:(2 * p + h + 1) * GLA_DV]
            qd_h = jnp.where(mh, qd_p, 0.0)
            att = jnp.where(incl, _dot1(qd_h, ki_p, NT), 0.0)
            outs.append(_dot1(att, vh) + _dot3(qd_h, s))
            upd.append(_dot1(ke_p, vh, TN))
        s_ref[p] = dec_col * s + jnp.where(top, upd[0], upd[1])
    return outs


def _gla_chunk_kernel(qf, kf, vf, alf, qb, kb, vb, alb, a2_ref, ab_ref, of_ref, ob_ref, s_ref):
    @pl.when(pl.program_id(0) == 0)
    def _():
        s_ref[...] = jnp.zeros(s_ref.shape, F32)

    for z, (q, k, v, al, o_ref) in enumerate(((qf, kf, vf, alf, of_ref), (qb, kb, vb, alb, ob_ref))):
        outs = _gla_chunk_dir(q[...], k[...], v, al[...], a2_ref, ab_ref, s_ref.at[z], z, bool(z))
        for h, o in enumerate(outs):
            o_ref[:, h * GLA_DV:(h + 1) * GLA_DV] = o


def _gla_chunk(p_gla, a2p, ab, n_ctx):
    n = p_gla.shape[0]
    n_chunks, fwd, bwd = _chunk_maps(n, n_ctx)
    hk = GLA_HEADS * GLA_DK

    def specs(cm):
        return [pl.BlockSpec((CHUNK, hk), lambda i: (cm(i), 0)),
                pl.BlockSpec((CHUNK, hk), lambda i: (cm(i), 1)),
                pl.BlockSpec((CHUNK, BRANCH_W), lambda i: (cm(i), 1)),
                pl.BlockSpec((CHUNK, LANES), lambda i: (cm(i), (2 * hk + 2 * BRANCH_W) // LANES))]

    full = lambda a: pl.BlockSpec(a.shape, lambda i: (0,) * a.ndim)
    shape = jax.ShapeDtypeStruct((n, BRANCH_W), F32)
    return pl.pallas_call(
        _gla_chunk_kernel,
        grid=(n_chunks,),
        in_specs=specs(fwd) + specs(bwd) + [full(a2p), full(ab)],
        out_specs=[pl.BlockSpec((CHUNK, BRANCH_W), lambda i: (fwd(i), 0)),
                   pl.BlockSpec((CHUNK, BRANCH_W), lambda i: (bwd(i), 0))],
        out_shape=[shape, shape],
        scratch_shapes=[pltpu.VMEM((2, GLA_HEADS // 2, LANES, LANES), F32)],
        compiler_params=_params(("arbitrary",)),
        name="gla_chunk",
    )(p_gla, p_gla, p_gla, p_gla, p_gla, p_gla, p_gla, p_gla, a2p, ab)


def _gated_norm_kernel(of_ref, ob_ref, gate_ref, g_ref, o_ref):
    for h in range(BRANCH_W // LANES):
        sl = slice(h * LANES, (h + 1) * LANES)
        o = of_ref[:, sl] + ob_ref[:, sl]
        ms = jnp.mean(o * o, axis=-1, keepdims=True)
        o_ref[:, sl] = o * lax.rsqrt(ms + NORM_EPS) * g_ref[...] * _silu(gate_ref[:, sl])


def _gated_norm(of, ob, gate_src, gate_block, norm_g, tb, name):
    n = of.shape[0]
    spec = pl.BlockSpec((tb, BRANCH_W), lambda i: (i, 0))
    return pl.pallas_call(
        _gated_norm_kernel,
        grid=(n // tb,),
        in_specs=[spec, spec, pl.BlockSpec((tb, BRANCH_W), lambda i: (i, gate_block)),
                  pl.BlockSpec((1, LANES), lambda i: (0, 0))],
        out_specs=spec,
        out_shape=jax.ShapeDtypeStruct((n, BRANCH_W), F32),
        compiler_params=_params(("parallel",)),
        name=name,
    )(of, ob, gate_src, norm_g)


def _gdn_prep_kernel(cur_ref, prev_ref, next_ref, sc_ref, cw_ref, gp_ref,
                     q_ref, k_ref, v_ref, gb_ref, ext_ref, *, tb, ctx_blocks, nblk):
    i = pl.program_id(0)
    seg_first = jnp.logical_or(i == 0, i == ctx_blocks)
    seg_last = jnp.logical_or(i == ctx_blocks - 1, i == nblk - 1)
    _fill_ext(ext_ref, cur_ref, prev_ref, next_ref, jnp.logical_not(seg_first),
              jnp.logical_not(seg_last), tb)
    pad = GDN_CONV // 2
    acc = cw_ref[0:1, :] * ext_ref[pl.ds(SUBLANES - pad, tb), :]
    for t in range(1, GDN_CONV):
        acc = acc + cw_ref[t:t + 1, :] * ext_ref[pl.ds(SUBLANES - pad + t, tb), :]
    qkv = _silu(acc)
    for h in range(GDN_HEADS):
        for j, ref in enumerate((q_ref, k_ref, v_ref)):
            sl = slice(j * BRANCH_W + h * GDN_HEAD, j * BRANCH_W + (h + 1) * GDN_HEAD)
            osl = slice(h * GDN_HEAD, (h + 1) * GDN_HEAD)
            x = qkv[:, sl]
            if j < 2:
                x = x * lax.rsqrt(jnp.sum(x * x, axis=-1, keepdims=True) + 1e-12)
            if j == 0:
                x = x * (GDN_HEAD ** -0.5)
            ref[:, osl] = x
    sc = sc_ref[...]
    lane = _iota(sc.shape, 1)
    nh2 = 2 * GDN_HEADS
    g = -jnp.exp(gp_ref[0:1, :]) * _softplus(sc + gp_ref[1:2, :])
    beta = _sigmoid(sc)
    gb_ref[...] = jnp.where(lane < nh2, g, jnp.where(lane < 2 * nh2, beta, 0.0))


def _gdn_prep(p_gdn, conv_w, gparams, n_ctx, tb):
    n = p_gdn.shape[0]
    nblk = n // tb
    qkv_cols = 3 * BRANCH_W
    r, prev_map, next_map = _halo_specs(tb, qkv_cols)
    sc_block = (4 * BRANCH_W) // LANES
    full = lambda a: pl.BlockSpec(a.shape, lambda i: (0,) * a.ndim)
    shape = jax.ShapeDtypeStruct((n, BRANCH_W), F32)
    spec = pl.BlockSpec((tb, BRANCH_W), lambda i: (i, 0))
    return pl.pallas_call(
        functools.partial(_gdn_prep_kernel, tb=tb, ctx_blocks=n_ctx // tb, nblk=nblk),
        grid=(nblk,),
        in_specs=[pl.BlockSpec((tb, qkv_cols), lambda i: (i, 0)),
                  pl.BlockSpec((SUBLANES, qkv_cols), prev_map),
                  pl.BlockSpec((SUBLANES, qkv_cols), functools.partial(next_map, nblk=nblk)),
                  pl.BlockSpec((tb, LANES), lambda i: (i, sc_block)),
                  full(conv_w), full(gparams)],
        out_specs=[spec, spec, spec, pl.BlockSpec((tb, LANES), lambda i: (i, 0))],
        out_shape=[shape, shape, shape, jax.ShapeDtypeStruct((n, LANES), F32)],
        scratch_shapes=[pltpu.VMEM((tb + 2 * SUBLANES, qkv_cols), F32)],
        compiler_params=_params(("parallel",)),
        name="gdn_prep",
    )(p_gdn, p_gdn, p_gdn, p_gdn, conv_w, gparams)


def _gdn_chunk_dir(q_ref, k_ref, v_ref, gb, s_ref, z, rev):
    n = CHUNK
    incl, strict = _order_masks(n, rev)
    tri = incl.astype(BF16)
    row = _iota((n, n), 0)
    col = _iota((n, n), 1)
    incl_t = (row >= col) if rev else (row <= col)
    ones = jnp.ones((n, n), BF16)
    outs = []
    for h in range(GDN_HEADS):
        sl = slice(h * GDN_HEAD, (h + 1) * GDN_HEAD)
        q, k, v = q_ref[:, sl], k_ref[:, sl], v_ref[:, sl]
        j = z * GDN_HEADS + h
        g_b = jnp.broadcast_to(gb[:, j:j + 1], (n, LANES))
        beta_b = jnp.broadcast_to(gb[:, 2 * GDN_HEADS + j:2 * GDN_HEADS + j + 1], (n, LANES))
        gam = _dotc(tri, g_b)
        gam_row = _dotc(ones, jnp.where(incl_t, g_b[:, :n], 0.0))
        gam_end = gam[0:1, :] if rev else gam[n - 1:n, :]
        decay = jnp.where(incl, jnp.exp(jnp.where(incl, gam[:, :n] - gam_row, 0.0)), 0.0)
        e_gam = jnp.exp(gam)
        kk = _dot1(k, k, NT)
        a = jnp.where(strict, beta_b[:, :n] * kk * decay, 0.0)
        t = _tri_inv(a, _dot3)
        u = _dot3(t, beta_b * v)
        wk = _dot3(t, beta_b * e_gam * k)
        qk = _dot1(q, k, NT) * decay
        q_dec = q * e_gam
        k_end = k * jnp.exp(gam_end - gam)
        dec_end = jnp.exp(gam_end)
        s = s_ref[h]
        y0 = _dot1(qk, u)
        qp = q_dec - _dot1(qk, wk)
        outs.append(y0 + _dot3(qp, s))
        p_mat = jnp.where(_iota((LANES, LANES), 0) == _iota((LANES, LANES), 1),
                          jnp.broadcast_to(dec_end, (LANES, LANES)), 0.0) - _dot1(k_end, wk, TN)
        s_ref[h] = _dot3(p_mat, s) + _dot1(k_end, u, TN)
    return outs


def _gdn_chunk_kernel(qf, kf, vf, gbf, qb, kb, vb, gbb, of_ref, ob_ref, s_ref):
    @pl.when(pl.program_id(0) == 0)
    def _():
        s_ref[...] = jnp.zeros(s_ref.shape, F32)

    for z, (q, k, v, gb, o_ref) in enumerate(((qf, kf, vf, gbf, of_ref), (qb, kb, vb, gbb, ob_ref))):
        outs = _gdn_chunk_dir(q, k, v, gb[...], s_ref.at[z], z, bool(z))
        for h, o in enumerate(outs):
            o_ref[:, h * GDN_HEAD:(h + 1) * GDN_HEAD] = o


def _gdn_chunk(q, k, v, gb, n_ctx):
    n = q.shape[0]
    n_chunks, fwd, bwd = _chunk_maps(n, n_ctx)

    def specs(cm):
        wide = pl.BlockSpec((CHUNK, BRANCH_W), lambda i: (cm(i), 0))
        return [wide, wide, wide, pl.BlockSpec((CHUNK, LANES), lambda i: (cm(i), 0))]

    shape = jax.ShapeDtypeStruct((n, BRANCH_W), F32)
    return pl.pallas_call(
        _gdn_chunk_kernel,
        grid=(n_chunks,),
        in_specs=specs(fwd) + specs(bwd),
        out_specs=[pl.BlockSpec((CHUNK, BRANCH_W), lambda i: (fwd(i), 0)),
                   pl.BlockSpec((CHUNK, BRANCH_W), lambda i: (bwd(i), 0))],
        out_shape=[shape, shape],
        scratch_shapes=[pltpu.VMEM((2, GDN_HEADS, GDN_HEAD, GDN_HEAD), F32)],
        compiler_params=_params(("arbitrary",)),
        name="gdn_chunk",
    )(q, k, v, gb, q, k, v, gb)


def _merge_kernel(hn_ref, yrw_ref, ygla_ref, ygdn_ref, h_ref, wg_ref, wb_ref, wo_ref, mod_ref, o_ref,
                  *, n_ctx, tb):
    hn = hn_ref[...]
    merged = jnp.zeros((tb, D_MODEL), F32)
    for g, y_ref in enumerate((yrw_ref, ygla_ref, ygdn_ref)):
        gate = _sigmoid(_dg(hn, wg_ref[:, g * D_MODEL:(g + 1) * D_MODEL]))
        merged = merged + _dg(y_ref[...].astype(BF16), wb_ref[g]) * gate
    mix = _dg(merged.astype(BF16), wo_ref[...])
    gate_vec = _seg_vec(mod_ref, 2, pl.program_id(0) * tb, tb, n_ctx)
    o_ref[...] = h_ref[...] + mix * gate_vec


def _merge(hn, y_rw, y_gla, y_gdn, h, wg, wb, wo, mod, n_ctx, tb):
    n = h.shape[0]
    full = lambda a: pl.BlockSpec(a.shape, lambda i: (0,) * a.ndim)
    wide = pl.BlockSpec((tb, D_MODEL), lambda i: (i, 0))
    half = pl.BlockSpec((tb, BRANCH_W), lambda i: (i, 0))
    return pl.pallas_call(
        functools.partial(_merge_kernel, n_ctx=n_ctx, tb=tb),
        grid=(n // tb,),
        in_specs=[wide, half, half, half, wide, full(wg), full(wb), full(wo), full(mod)],
        out_specs=wide,
        out_shape=jax.ShapeDtypeStruct((n, D_MODEL), F32),
        compiler_params=_params(("parallel",)),
        name="merge_out",
    )(hn, y_rw, y_gla, y_gdn, h, wg, wb, wo, mod)


def _mlp_kernel(h_ref, g_ref, mod_ref, w1_ref, w2_ref, o_ref, hn_ref, acc_ref, *, n_ctx, tb, nj):
    j = pl.program_id(1)
    row0 = pl.program_id(0) * tb

    @pl.when(j == 0)
    def _():
        xn = _rmsnorm_rows(h_ref[...], g_ref[...])
        xn = xn * (1.0 + _seg_vec(mod_ref, 4, row0, tb, n_ctx)) + _seg_vec(mod_ref, 3, row0, tb, n_ctx)
        hn_ref[...] = xn.astype(BF16)
        acc_ref[...] = jnp.zeros(acc_ref.shape, F32)

    hid = jnp.maximum(_dg(hn_ref[...], w1_ref[...]), 0.0)
    acc_ref[...] += _dg((hid * hid).astype(BF16), w2_ref[...])

    @pl.when(j == nj - 1)
    def _():
        o_ref[...] = h_ref[...] + acc_ref[...] * _seg_vec(mod_ref, 5, row0, tb, n_ctx)


def _mlp(h, g, mod, w1, w2, n_ctx, tb, hb):
    n = h.shape[0]
    nj = MLP_HIDDEN // hb
    wide = pl.BlockSpec((tb, D_MODEL), lambda i, j: (i, 0))
    return pl.pallas_call(
        functools.partial(_mlp_kernel, n_ctx=n_ctx, tb=tb, nj=nj),
        grid=(n // tb, nj),
        in_specs=[wide, pl.BlockSpec((1, D_MODEL), lambda i, j: (0, 0)),
                  pl.BlockSpec(mod.shape, lambda i, j: (0, 0)),
                  pl.BlockSpec((D_MODEL, hb), lambda i, j: (0, j)),
                  pl.BlockSpec((hb, D_MODEL), lambda i, j: (j, 0))],
        out_specs=wide,
        out_shape=jax.ShapeDtypeStruct((n, D_MODEL), F32),
        scratch_shapes=[pltpu.VMEM((tb, D_MODEL), BF16), pltpu.VMEM((tb, D_MODEL), F32)],
        compiler_params=_params(("parallel", "arbitrary")),
        name="mlp",
    )(h, g, mod, w1, w2)


def _block_diag_ones(n, blk):
    r = jnp.arange(n) // blk
    return (r[:, None] == r[None, :]).astype(BF16)


def _pad_cols(w, cols):
    return jnp.pad(w, ((0, 0), (0, cols - w.shape[1])))


def _lora_pad(w2, lora):
    out = jnp.zeros((2, LANES, w2.shape[2]), F32)
    for z in range(2):
        out = out.at[z, z * lora:(z + 1) * lora].set(w2[z])
    return out


def _pick_block(n, n_ctx, candidates):
    for c in candidates:
        if n % c == 0 and n_ctx % c == 0:
            return c
    raise ValueError("token counts must be multiples of the chunk length")


def _pick_rows(n, candidates):
    for c in candidates:
        if n % c == 0:
            return c
    raise ValueError("unsupported token count")


def _to_colmajor(t, n_ctx, rows):
    lat = t[n_ctx:]
    lat = lat.reshape((rows, GRID_W) + lat.shape[1:]).swapaxes(0, 1).reshape(lat.shape)
    return jnp.concatenate([t[:n_ctx], lat], axis=0)


def _from_colmajor(t, n_ctx, rows):
    lat = t[n_ctx:]
    lat = lat.reshape((GRID_W, rows) + lat.shape[1:]).swapaxes(0, 1).reshape(lat.shape)
    return jnp.concatenate([t[:n_ctx], lat], axis=0)


def kernel(x, c, ctx, c_ctx, w_mod, b_mod, norm1_g, w_in, rw_mu, rw_w0, rw_w2, rw_a0, rw_a2, rw_g2, rw_kk, rw_ka, rw_rk, rw_ln_w, rw_ln_b, gla_a2, gla_ab, gla_norm_g, gdn_conv, gdn_a_log, gdn_dt_bias, gdn_norm_g, w_branch, w_out, norm2_g, w_mlp1, w_mlp2, final_g):
    bsz, n_lat, dm = x.shape
    assert bsz == 1 and dm == D_MODEL
    n_ctx = ctx.shape[1]
    n = n_ctx + n_lat
    rows = n_lat // GRID_W
    depth = w_in.shape[0]
    tb =_pick_block(n, n_ctx, (256, 128, 64))
    tm = _pick_rows(n, (640, 512, 256, 128, 64))
    tmlp = _pick_rows(n, (640, 512, 256, 128, 64))

    cc = jnp.zeros((SUBLANES, D_MODEL), F32).at[0].set(c_ctx).at[1].set(c[0])
    mod_all = _modulation(cc, w_mod, b_mod)

    bd64 = _block_diag_ones(BRANCH_W, RW_HEAD)
    bd128 = _block_diag_ones(LANES, RW_HEAD)
    h = jnp.concatenate([ctx[0], x[0]], axis=0)

    for l in range(depth):
        mod = mod_all[l]
        o0 = 0
        w_rw = w_in[l][:, o0:o0 + RW_COLS].astype(BF16)
        o0 += RW_COLS
        w_gla = _pad_cols(w_in[l][:, o0:o0 + GLA_COLS], GLA_COLS_PAD).astype(BF16)
        o0 += GLA_COLS
        w_gdn = _pad_cols(w_in[l][:, o0:o0 + GDN_COLS], GDN_COLS_PAD).astype(BF16)
        o0 += GDN_COLS
        w_gate = w_in[l][:, o0:].astype(BF16)

        hn = _norm(h, norm1_g[l][None], mod, n_ctx, tb, BF16)
        hn_cm = _to_colmajor(hn, n_ctx, rows)
        p_rw = _project(hn, w_rw, tm, "proj_rwkv")
        p_gla = _project(hn_cm, w_gla, tm, "proj_gla")
        p_gdn = _project(hn, w_gdn, tm, "proj_gdn")

        mu = rw_mu[l]
        rw_wts = {
            "mu3": jnp.zeros((SUBLANES, RW_COLS), F32).at[0].set(mu[0]).at[1].set(1 - mu[0] - mu[1]).at[2].set(mu[1]),
            "w0": rw_w0[l], "w2p": _lora_pad(rw_w2[l], RW_DECAY_LORA),
            "a0": rw_a0[l], "a2p": _lora_pad(rw_a2[l], RW_A_LORA), "g2": rw_g2[l],
            "kka": jnp.zeros((SUBLANES, BRANCH_W), F32).at[0].set(rw_kk[l]).at[1].set(rw_ka[l])
                      .at[2].set(rw_rk[l].reshape(-1)),
            "bd64": bd64,
        }
        r, v, kk, lw, kd, bdir, g, bonus = _rw_prep(p_rw, rw_wts, n_ctx, tb)
        yf, yb = _rw_chunk(r, v, kk, lw, kd, bdir, n_ctx)
        ln = jnp.zeros((SUBLANES, BRANCH_W), F32).at[0].set(rw_ln_w[l]).at[1].set(rw_ln_b[l])
        y_rw = _rw_post(yf, yb, bonus, g, ln, bd128, tb)

        of, ob = _gla_chunk(p_gla, _lora_pad(gla_a2[l], GLA_GATE_LORA), gla_ab[l], n_ctx)
        y_gla_cm = _gated_norm(of, ob, p_gla, 2, gla_norm_g[l][None], tb, "gla_post")
        y_gla = _from_colmajor(y_gla_cm, n_ctx, rows)

        conv_w = jnp.zeros((SUBLANES, 3 * BRANCH_W), F32).at[:GDN_CONV].set(gdn_conv[l])
        nh2 = 2 * GDN_HEADS
        gparams = jnp.zeros((SUBLANES, LANES), F32).at[0, :nh2].set(gdn_a_log[l].reshape(-1)) \
            .at[1, :nh2].set(gdn_dt_bias[l].reshape(-1))
        q_d, k_d, v_d, gb = _gdn_prep(p_gdn, conv_w, gparams, n_ctx, tb)
        of, ob = _gdn_chunk(q_d, k_d, v_d, gb, n_ctx)
        y_gdn = _gated_norm(of, ob, p_gdn, 3, gdn_norm_g[l][None], tb, "gdn_post")

        h = _merge(hn, y_rw, y_gla, y_gdn, h, w_gate, w_branch[l].astype(BF16), w_out[l].astype(BF16),
                   mod, n_ctx, tb)
        h = _mlp(h, norm2_g[l][None], mod, w_mlp1[l].astype(BF16), w_mlp2[l].astype(BF16), n_ctx, tmlp, 1024)

    out = _norm(h, final_g[None], mod_all[0], n_ctx, tb, F32, shift_idx=None, scale_idx=None)
    return out[n_ctx:][None]
```

```python
import functools

import jax
import jax.numpy as jnp
from jax import lax
from jax.experimental import pallas as pl
from jax.experimental.pallas import tpu as pltpu

F32 = jnp.float32
BF16 = jnp.bfloat16

D_MODEL = 1024
GRID_W = 64
CHUNK = 64
BRANCH_W = D_MODEL // 2
NORM_EPS = 1e-6
RW_HEAD = 64
RW_HEADS = BRANCH_W // RW_HEAD
RW_PAIRS = RW_HEADS // 2
RW_DECAY_LORA = 64
RW_A_LORA = 64
RW_G_LORA = 128
RW_GN_EPS = 64e-5
GLA_HEADS = 4
GLA_DK = 64
GLA_DV = BRANCH_W // GLA_HEADS
GLA_GATE_LORA = 16
GLA_GATE_NORM = 16.0
GDN_HEADS = 4
GDN_HEAD = BRANCH_W // GDN_HEADS
GDN_CONV = 5
MLP_HIDDEN = 4 * D_MODEL
RW_COLS = 3 * BRANCH_W + 2 * RW_DECAY_LORA + 2 * RW_A_LORA + RW_G_LORA
GLA_COLS = 2 * GLA_HEADS * GLA_DK + 2 * BRANCH_W + 2 * GLA_GATE_LORA
GDN_COLS = 4 * BRANCH_W + 4 * GDN_HEADS

LANES = 128
SUBLANES = 8
GLA_COLS_PAD = -(-GLA_COLS // LANES) * LANES
GDN_COLS_PAD = -(-GDN_COLS // LANES) * LANES
VMEM_LIMIT_BYTES = 48 * 1024 * 1024

NN = (((1,), (0,)), ((), ()))
NT = (((1,), (1,)), ((), ()))
TN = (((0,), (0,)), ((), ()))


def _dg(a, b, dims=NN):
    return lax.dot_general(a, b, dims, preferred_element_type=F32)


def _dot1(a, b, dims=NN):
    return _dg(a.astype(BF16), b.astype(BF16), dims)


def _split2(a):
    hi = a.astype(BF16)
    lo = (a - hi.astype(F32)).astype(BF16)
    return hi, lo


def _dot3(a, b, dims=NN):
    ah, al = _split2(a)
    bh, bl = _split2(b)
    return _dg(ah, bh, dims) + (_dg(ah, bl, dims) + _dg(al, bh, dims))


def _dotc(c_bf16, x, dims=NN):
    h1 = x.astype(BF16)
    r1 = x - h1.astype(F32)
    h2 = r1.astype(BF16)
    h3 = (r1 - h2.astype(F32)).astype(BF16)
    return _dg(c_bf16, h1, dims) + (_dg(c_bf16, h2, dims) + _dg(c_bf16, h3, dims))


def _split3(x):
    h1 = x.astype(BF16)
    r1 = x - h1.astype(F32)
    h2 = r1.astype(BF16)
    h3 = (r1 - h2.astype(F32)).astype(BF16)
    return h1, h2, h3


def _dotxc(x, c_bf16, dims=NN):
    h1 = x.astype(BF16)
    r1 = x - h1.astype(F32)
    h2 = r1.astype(BF16)
    h3 = (r1 - h2.astype(F32)).astype(BF16)
    return _dg(h1, c_bf16, dims) + (_dg(h2, c_bf16, dims) + _dg(h3, c_bf16, dims))


def _iota(shape, axis):
    return lax.broadcasted_iota(jnp.int32, shape, axis)


def _order_masks(n, rev):
    row = _iota((n, n), 0)
    col = _iota((n, n), 1)
    if rev:
        return col >= row, col > row
    return col <= row, col < row


def _sigmoid(x):
    return 1.0 / (1.0 + jnp.exp(-x))


def _softplus(x):
    return jnp.maximum(x, 0.0) + jnp.log(1.0 + jnp.exp(-jnp.abs(x)))


def _silu(x):
    return x * _sigmoid(x)


def _params(sem):
    return pltpu.CompilerParams(dimension_semantics=sem, vmem_limit_bytes=VMEM_LIMIT_BYTES)


def _mod_kernel(cc_ref, w_ref, b_ref, o_ref):
    s = _silu(cc_ref[...])
    o_ref[...] = _dot3(s, w_ref[...]) + b_ref[...]


def _modulation(cc, w_mod, b_mod):
    depth = w_mod.shape[0]
    ncol = w_mod.shape[2] // D_MODEL
    return pl.pallas_call(
        _mod_kernel,
        grid=(depth, ncol),
        in_specs=[
            pl.BlockSpec((SUBLANES, D_MODEL), lambda l, j: (0, 0)),
            pl.BlockSpec((None, D_MODEL, D_MODEL), lambda l, j: (l, 0, j)),
            pl.BlockSpec((None, 1, D_MODEL), lambda l, j: (l, 0, j)),
        ],
        out_specs=pl.BlockSpec((None, SUBLANES, D_MODEL), lambda l, j: (l, 0, j)),
        out_shape=jax.ShapeDtypeStruct((depth, SUBLANES, w_mod.shape[2]), F32),
        compiler_params=_params(("parallel", "parallel")),
        name="modulation",
    )(cc, w_mod, b_mod.reshape(depth, 1, -1))


def _seg_vec(mod_ref, idx, row0, nrows, n_ctx):
    lo = idx * D_MODEL
    ctx = mod_ref[0:1, lo:lo + D_MODEL]
    lat = mod_ref[1:2, lo:lo + D_MODEL]
    is_ctx = (_iota((nrows, 1), 0) + row0) < n_ctx
    return jnp.where(is_ctx, ctx, lat)


def _rmsnorm_rows(x, g):
    ms = jnp.mean(x * x, axis=-1, keepdims=True)
    return x * lax.rsqrt(ms + NORM_EPS) * g


def _norm_kernel(x_ref, g_ref, mod_ref, o_ref, *, n_ctx, tb, shift_idx, scale_idx):
    row0 = pl.program_id(0) * tb
    xn = _rmsnorm_rows(x_ref[...], g_ref[...])
    if scale_idx is not None:
        xn = xn * (1.0 + _seg_vec(mod_ref, scale_idx, row0, tb, n_ctx)) \
            + _seg_vec(mod_ref, shift_idx, row0, tb, n_ctx)
    o_ref[...] = xn.astype(o_ref.dtype)


def _norm(x, g, mod, n_ctx, tb, out_dtype, shift_idx=0, scale_idx=1):
    n = x.shape[0]
    return pl.pallas_call(
        functools.partial(_norm_kernel, n_ctx=n_ctx, tb=tb, shift_idx=shift_idx, scale_idx=scale_idx),
        grid=(n // tb,),
        in_specs=[
            pl.BlockSpec((tb, D_MODEL), lambda i: (i, 0)),
            pl.BlockSpec((1, D_MODEL), lambda i: (0, 0)),
            pl.BlockSpec(mod.shape, lambda i: (0, 0)),
        ],
        out_specs=pl.BlockSpec((tb, D_MODEL), lambda i: (i, 0)),
        out_shape=jax.ShapeDtypeStruct((n, D_MODEL), out_dtype),
        compiler_params=_params(("parallel",)),
        name="adaln_norm",
    )(x, g, mod)


def _mm_kernel(a_ref, w_ref, o_ref):
    o_ref[...] = _dg(a_ref[...], w_ref[...])


def _project(a, w, tm, name):
    n, k = a.shape
    cols = w.shape[1]
    return pl.pallas_call(
        _mm_kernel,
        grid=(n // tm,),
        in_specs=[pl.BlockSpec((tm, k), lambda i: (i, 0)),
                  pl.BlockSpec((k, cols), lambda i: (0, 0))],
        out_specs=pl.BlockSpec((tm, cols), lambda i: (i, 0)),
        out_shape=jax.ShapeDtypeStruct((n, cols), F32),
        compiler_params=_params(("parallel",)),
        name=name,
    )(a, w)


def _fill_ext(ext_ref, cur_ref, prev_ref, next_ref, has_prev, has_next, tb):
    zeros = jnp.zeros(prev_ref.shape, F32)
    ext_ref[0:SUBLANES, :] = jnp.where(has_prev, prev_ref[...], zeros)
    ext_ref[SUBLANES:SUBLANES + tb, :] = cur_ref[...]
    ext_ref[SUBLANES + tb:SUBLANES + tb + SUBLANES, :] = jnp.where(has_next, next_ref[...], zeros)


def _halo_specs(tb, cols, col_block=0):
    r = tb // SUBLANES

    def prev_map(i):
        return (jnp.maximum(i * r - 1, 0), col_block)

    def next_map(i, nblk):
        return (jnp.minimum((i + 1) * r, nblk * r - 1), col_block)

    return r, prev_map, next_map


def _rw_prep_kernel(cur_ref, prev_ref, next_ref, mu_ref, w0_ref, w2_ref, a0_ref, a2_ref, g2_ref,
                    kka_ref, bd_ref,
                    r_ref, v_ref, kk_ref, lw_ref, kd_ref, bdir_ref, g_ref, bonus_ref,
                    ext_ref, *, tb, ctx_blocks, nblk):
    i = pl.program_id(0)
    seg_first = jnp.logical_or(i == 0, i == ctx_blocks)
    seg_last = jnp.logical_or(i == ctx_blocks - 1, i == nblk - 1)
    _fill_ext(ext_ref, cur_ref, prev_ref, next_ref, jnp.logical_not(seg_first),
              jnp.logical_not(seg_last), tb)
    xs = (mu_ref[0:1, :] * ext_ref[pl.ds(SUBLANES - 1, tb), :]
          + mu_ref[1:2, :] * ext_ref[pl.ds(SUBLANES, tb), :]
          + mu_ref[2:3, :] * ext_ref[pl.ds(SUBLANES + 1, tb), :])
    bw = BRANCH_W
    r = xs[:, 0:bw]
    k = xs[:, bw:2 * bw]
    v = xs[:, 2 * bw:3 * bw]
    wl = xs[:, 3 * bw:3 * bw + LANES]
    al = xs[:, 3 * bw + LANES:3 * bw + 2 * LANES]
    gl = xs[:, 3 * bw + 2 * LANES:3 * bw + 3 * LANES]
    bd = bd_ref[...]
    k_k = kka_ref[0:1, :]
    k_a = kka_ref[1:2, :]
    r_k = kka_ref[2:3, :]

    kk0 = k * k_k
    kk = kk0 * lax.rsqrt(_dotxc(kk0 * kk0, bd) + 1e-12)
    g = _dot3(_sigmoid(gl), g2_ref[...])
    twl = jnp.tanh(wl)
    ksum = jnp.zeros_like(k)
    for z in range(2):
        wlog = w0_ref[z:z + 1, :] + _dot3(twl, w2_ref[z])
        w = -_softplus(-wlog) - 0.5
        lw = -jnp.exp(w)
        a = _sigmoid(a0_ref[z:z + 1, :] + _dot3(al, a2_ref[z]))
        kd = k * (1.0 + (a - 1.0) * k_a)
        bdir = kk * a
        ksum = ksum + kd
        for p in range(RW_PAIRS):
            sl = slice(p * LANES, (p + 1) * LANES)
            lw_ref[z, p] = lw[:, sl]
            kd_ref[z, p] = kd[:, sl]
            bdir_ref[z, p] = bdir[:, sl]
    bonus = _dotxc(r * ksum * r_k, bd) * v
    for p in range(RW_PAIRS):
        sl = slice(p * LANES, (p + 1) * LANES)
        r_ref[p] = r[:, sl]
        v_ref[p] = v[:, sl]
        kk_ref[p] = kk[:, sl]
        g_ref[p] = g[:, sl]
        bonus_ref[p] = bonus[:, sl]


def _rw_prep(p_rw, wts, n_ctx, tb):
    n = p_rw.shape[0]
    nblk = n // tb
    r, prev_map, next_map = _halo_specs(tb, RW_COLS)
    full = lambda a: pl.BlockSpec(a.shape, lambda i: (0,) * a.ndim)
    pair_spec = pl.BlockSpec((RW_PAIRS, tb, LANES), lambda i: (0, i, 0))
    dir_spec = pl.BlockSpec((2, RW_PAIRS, tb, LANES), lambda i: (0, 0, i, 0))
    pair_shape = jax.ShapeDtypeStruct((RW_PAIRS, n, LANES), F32)
    dir_shape = jax.ShapeDtypeStruct((2, RW_PAIRS, n, LANES), F32)
    params = [wts["mu3"], wts["w0"], wts["w2p"], wts["a0"], wts["a2p"], wts["g2"], wts["kka"], wts["bd64"]]
    return pl.pallas_call(
        functools.partial(_rw_prep_kernel, tb=tb, ctx_blocks=n_ctx // tb, nblk=nblk),
        grid=(nblk,),
        in_specs=[pl.BlockSpec((tb, RW_COLS), lambda i: (i, 0)),
                  pl.BlockSpec((SUBLANES, RW_COLS), prev_map),
                  pl.BlockSpec((SUBLANES, RW_COLS), functools.partial(next_map, nblk=nblk))]
                 + [full(a) for a in params],
        out_specs=[pair_spec, pair_spec, pair_spec, dir_spec, dir_spec, dir_spec, pair_spec, pair_spec],
        out_shape=[pair_shape, pair_shape, pair_shape, dir_shape, dir_shape, dir_shape, pair_shape, pair_shape],
        scratch_shapes=[pltpu.VMEM((tb + 2 * SUBLANES, RW_COLS), F32)],
        compiler_params=_params(("parallel",)),
        name="rwkv_prep",
    )(p_rw, p_rw, p_rw, *params)


def _tri_inverse(a_list, dot):
    n = a_list[0].shape[0]
    row = _iota((n, n), 0)
    col = _iota((n, n), 1)
    eye = (row == col).astype(F32)
    t = [eye - jnp.where((row // 2) == (col // 2), a, 0.0) for a in a_list]
    b = 2
    while b < n:
        off = jnp.logical_and((row // (2 * b)) == (col // (2 * b)), (row // b) != (col // b))
        x = [dot(jnp.where(off, a, 0.0), ti) for a, ti in zip(a_list, t)]
        t = [ti - dot(ti, xi) for ti, xi in zip(t, x)]
        b *= 2
    return t


def _rw_chunk_group(tiles, m_refs, revs):
    n = CHUNK
    masks = {rev: _order_masks(n, rev) for rev in set(revs)}
    head_a = _iota((n, LANES), 1) < RW_HEAD
    row = _iota((LANES, LANES), 0)
    col = _iota((LANES, LANES), 1)
    same_head = (row < RW_HEAD) == (col < RW_HEAD)
    diag = row == col

    pre = []
    for (r, v, kk, lw, kd, bdir), rev in zip(tiles, revs):
        incl, _ = masks[rev]
        cum = _dotc(incl.astype(BF16), lw)
        cum_end = cum[0:1, :] if rev else cum[n - 1:n, :]
        e_inv = jnp.exp(-cum)
        e_end = jnp.exp(cum_end - cum)
        pre.append(dict(v=v, kkd=kk * jnp.exp(cum - lw), kinv=kd * e_inv, binv=bdir * e_inv,
                        rd=r * jnp.exp(cum), kend=kd * e_end, bend=bdir * e_end, g_end=jnp.exp(cum_end)))

    a_kb, a_kk, a_rb, a_rk = [], [], [], []
    for d, rev in zip(pre, revs):
        incl, strict = masks[rev]
        for h in range(2):
            mh = head_a if h == 0 else jnp.logical_not(head_a)
            kkd_h = jnp.where(mh, d["kkd"], 0.0)
            rd_h = jnp.where(mh, d["rd"], 0.0)
            a_kb.append(jnp.where(strict, _dot1(kkd_h, d["binv"], NT), 0.0))
            a_kk.append(jnp.where(strict, _dot1(kkd_h, d["kinv"], NT), 0.0))
            a_rb.append(jnp.where(incl, _dot1(rd_h, d["binv"], NT), 0.0))
            a_rk.append(jnp.where(incl, _dot1(rd_h, d["kinv"], NT), 0.0))
    t = _tri_inverse(a_kb, _dot1)
    inst = [(i, h) for i in range(len(pre)) for h in range(2)]
    uw = [_dot3(t[j], jnp.concatenate([_dot1(a_kk[j], pre[i]["v"]), pre[i]["kkd"]], axis=1))
          for j, (i, h) in enumerate(inst)]
    rb_uw = [_dot1(a_rb[j], uw[j]) for j in range(len(inst))]
    y0 = [_dot1(a_rk[j], pre[i]["v"]) - rb_uw[j][:, :LANES] for j, (i, h) in enumerate(inst)]
    qp = [pre[i]["rd"] - rb_uw[j][:, LANES:] for j, (i, h) in enumerate(inst)]

    ys = []
    for i, d in enumerate(pre):
        ja, jb = 2 * i, 2 * i + 1
        uw_i = jnp.where(jnp.concatenate([head_a, head_a], axis=1), uw[ja], uw[jb])
        y0_i = jnp.where(head_a, y0[ja], y0[jb])
        qp_i = jnp.where(head_a, qp[ja], qp[jb])
        m = m_refs[i][...]
        ys.append(y0_i + _dot3(qp_i, m))
        b_uw = _dot1(d["bend"], uw_i, TN)
        p_mat = jnp.where(diag, jnp.broadcast_to(d["g_end"], (LANES, LANES)), 0.0) - b_uw[:, LANES:]
        z0 = _dot1(d["kend"], d["v"], TN) - b_uw[:, :LANES]
        m_refs[i][...] = jnp.where(same_head, _dot3(jnp.where(same_head, p_mat, 0.0), m) + z0, 0.0)
    return ys


RW_GROUP_PAIRS = 4


def _rw_chunk_kernel(rf, vf, kkf, lwf, kdf, bf, rb, vb, kkb, lwb, kdb, bb, yf_ref, yb_ref, m_ref):
    @pl.when(pl.program_id(0) == 0)
    def _():
        m_ref[...] = jnp.zeros(m_ref.shape, F32)

    dirs = ((rf, vf, kkf, lwf, kdf, bf, yf_ref, False), (rb, vb, kkb, lwb, kdb, bb, yb_ref, True))
    for p0 in range(0, RW_PAIRS, RW_GROUP_PAIRS):
        tiles, m_refs, revs, outs = [], [], [], []
        for z, (r, v, kk, lw, kd, b, y_ref, rev) in enumerate(dirs):
            for p in range(p0, p0 + RW_GROUP_PAIRS):
                tiles.append((r[p], v[p], kk[p], lw[p], kd[p], b[p]))
                m_refs.append(m_ref.at[z, p])
                revs.append(rev)
                outs.append((y_ref, p))
        for (y_ref, p), y in zip(outs, _rw_chunk_group(tiles, m_refs, revs)):
            y_ref[p] = y


def _chunk_maps(n, n_ctx):
    n_chunks = n // CHUNK
    ncc = n_ctx // CHUNK

    def fwd(i):
        return i

    def bwd(i):
        return jnp.where(i < ncc, ncc - 1 - i, n_chunks + ncc - 1 - i)

    return n_chunks, fwd, bwd


def _rw_chunk(r, v, kk, lw, kd, bdir, n_ctx):
    n = r.shape[1]
    n_chunks, fwd, bwd = _chunk_maps(n, n_ctx)

    def shared(cm):
        return pl.BlockSpec((RW_PAIRS, CHUNK, LANES), lambda i: (0, cm(i), 0))

    def per_dir(z, cm):
        return pl.BlockSpec((None, RW_PAIRS, CHUNK, LANES), lambda i: (z, 0, cm(i), 0))

    in_specs = [shared(fwd), shared(fwd), shared(fwd), per_dir(0, fwd), per_dir(0, fwd), per_dir(0, fwd),
                shared(bwd), shared(bwd), shared(bwd), per_dir(1, bwd), per_dir(1, bwd), per_dir(1, bwd)]
    shape = jax.ShapeDtypeStruct((RW_PAIRS, n, LANES), F32)
    return pl.pallas_call(
        _rw_chunk_kernel,
        grid=(n_chunks,),
        in_specs=in_specs,
        out_specs=[shared(fwd), shared(bwd)],
        out_shape=[shape, shape],
        scratch_shapes=[pltpu.VMEM((2, RW_PAIRS, LANES, LANES), F32)],
        compiler_params=_params(("arbitrary",)),
        name="rwkv_chunk",
    )(r, v, kk, lw, kd, bdir, r, v, kk, lw, kd, bdir)


def _rw_post_kernel(yf_ref, yb_ref, bonus_ref, g_ref, ln_ref, bd_ref, o_ref):
    bd = bd_ref[...]
    inv_n = 1.0 / RW_HEAD
    for p in range(RW_PAIRS):
        sl = slice(p * LANES, (p + 1) * LANES)
        y = yf_ref[p] + yb_ref[p]
        mu = _dotxc(y, bd) * inv_n
        d = y - mu
        var = _dotxc(d * d, bd) * inv_n
        yn = d * lax.rsqrt(var + RW_GN_EPS) * ln_ref[0:1, sl] + ln_ref[1:2, sl]
        o_ref[:, sl] = (yn + bonus_ref[p]) * g_ref[p]


def _rw_post(yf, yb, bonus, g, ln, bd128, tb):
    n = yf.shape[1]
    pair_spec = pl.BlockSpec((RW_PAIRS, tb, LANES), lambda i: (0, i, 0))
    full = lambda a: pl.BlockSpec(a.shape, lambda i: (0,) * a.ndim)
    return pl.pallas_call(
        _rw_post_kernel,
        grid=(n // tb,),
        in_specs=[pair_spec, pair_spec, pair_spec, pair_spec, full(ln), full(bd128)],
        out_specs=pl.BlockSpec((tb, BRANCH_W), lambda i: (i, 0)),
        out_shape=jax.ShapeDtypeStruct((n, BRANCH_W), F32),
        compiler_params=_params(("parallel",)),
        name="rwkv_post",
    )(yf, yb, bonus, g, ln, bd128)


def _gla_chunk_dir(q, k, v_ref, al, a2_ref, ab_ref, s_ref, z, rev):
    n = CHUNK
    incl, _ = _order_masks(n, rev)
    tri = incl.astype(BF16)
    zlog = _dot3(al, a2_ref[z]) + ab_ref[z:z + 1, :]
    log_a = (jnp.minimum(zlog, 0.0) - jnp.log(1.0 + jnp.exp(-jnp.abs(zlog)))) * (1.0 / GLA_GATE_NORM)
    bcum = _dotc(tri, log_a)
    b_end = bcum[0:1, :] if rev else bcum[n - 1:n, :]
    q_dec = q * (GLA_DK ** -0.5) * jnp.exp(bcum)
    k_inv = k * jnp.exp(-bcum)
    k_end = k * jnp.exp(b_end - bcum)
    ones = jnp.ones((n, LANES), BF16)
    head_a = _iota((n, LANES), 1) < GLA_DK
    top = _iota((LANES, LANES), 0) < GLA_DK
    outs = []
    for p in range(GLA_HEADS // 2):
        sl = slice(p * LANES, (p + 1) * LANES)
        qd_p, ki_p, ke_p = q_dec[:, sl], k_inv[:, sl], k_end[:, sl]
        dec_col = jnp.exp(_dotxc(log_a[:, sl], ones, TN))
        s = s_ref[p]
        upd = []
        for h in range(2):
            mh = head_a if h == 0 else jnp.logical_not(head_a)
            vh = v_ref[:, (2 * p + h) * GLA_DV:(2 * p + h + 1) * GLA_DV]
            qd_h = jnp.where(mh, qd_p, 0.0)
            att = jnp.where(incl, _dot1(qd_h, ki_p, NT), 0.0)
            outs.append(_dot1(att, vh) + _dot3(qd_h, s))
            upd.append(_dot1(ke_p, vh, TN))
        s_ref[p] = dec_col * s + jnp.where(top, upd[0], upd[1])
    return outs


def _gla_chunk_kernel(qf, kf, vf, alf, qb, kb, vb, alb, a2_ref, ab_ref, of_ref, ob_ref, s_ref):
    @pl.when(pl.program_id(0) == 0)
    def _():
        s_ref[...] = jnp.zeros(s_ref.shape, F32)

    for z, (q, k, v, al, o_ref) in enumerate(((qf, kf, vf, alf, of_ref), (qb, kb, vb, alb, ob_ref))):
        outs = _gla_chunk_dir(q[...], k[...], v, al[...], a2_ref, ab_ref, s_ref.at[z], z, bool(z))
        for h, o in enumerate(outs):
            o_ref[:, h * GLA_DV:(h + 1) * GLA_DV] = o


def _gla_chunk(p_gla, a2p, ab, n_ctx):
    n = p_gla.shape[0]
    n_chunks, fwd, bwd = _chunk_maps(n, n_ctx)
    hk = GLA_HEADS * GLA_DK

    def specs(cm):
        return [pl.BlockSpec((CHUNK, hk), lambda i: (cm(i), 0)),
                pl.BlockSpec((CHUNK, hk), lambda i: (cm(i), 1)),
                pl.BlockSpec((CHUNK, BRANCH_W), lambda i: (cm(i), 1)),
                pl.BlockSpec((CHUNK, LANES), lambda i: (cm(i), (2 * hk + 2 * BRANCH_W) // LANES))]

    full = lambda a: pl.BlockSpec(a.shape, lambda i: (0,) * a.ndim)
    shape = jax.ShapeDtypeStruct((n, BRANCH_W), F32)
    return pl.pallas_call(
        _gla_chunk_kernel,
        grid=(n_chunks,),
        in_specs=specs(fwd) + specs(bwd) + [full(a2p), full(ab)],
        out_specs=[pl.BlockSpec((CHUNK, BRANCH_W), lambda i: (fwd(i), 0)),
                   pl.BlockSpec((CHUNK, BRANCH_W), lambda i: (bwd(i), 0))],
        out_shape=[shape, shape],
        scratch_shapes=[pltpu.VMEM((2, GLA_HEADS // 2, LANES, LANES), F32)],
        compiler_params=_params(("arbitrary",)),
        name="gla_chunk",
    )(p_gla, p_gla, p_gla, p_gla, p_gla, p_gla, p_gla, p_gla, a2p, ab)


def _gated_norm_kernel(of_ref, ob_ref, gate_ref, g_ref, o_ref):
    for h in range(BRANCH_W // LANES):
        sl = slice(h * LANES, (h + 1) * LANES)
        o = of_ref[:, sl] + ob_ref[:, sl]
        ms = jnp.mean(o * o, axis=-1, keepdims=True)
        o_ref[:, sl] = o * lax.rsqrt(ms + NORM_EPS) * g_ref[...] * _silu(gate_ref[:, sl])


def _gated_norm(of, ob, gate_src, gate_block, norm_g, tb, name):
    n = of.shape[0]
    spec = pl.BlockSpec((tb, BRANCH_W), lambda i: (i, 0))
    return pl.pallas_call(
        _gated_norm_kernel,
        grid=(n // tb,),
        in_specs=[spec, spec, pl.BlockSpec((tb, BRANCH_W), lambda i: (i, gate_block)),
                  pl.BlockSpec((1, LANES), lambda i: (0, 0))],
        out_specs=spec,
        out_shape=jax.ShapeDtypeStruct((n, BRANCH_W), F32),
        compiler_params=_params(("parallel",)),
        name=name,
    )(of, ob, gate_src, norm_g)


def _gdn_prep_kernel(cur_ref, prev_ref, next_ref, sc_ref, cw_ref, gp_ref,
                     q_ref, k_ref, v_ref, gb_ref, ext_ref, *, tb, ctx_blocks, nblk):
    i = pl.program_id(0)
    seg_first = jnp.logical_or(i == 0, i == ctx_blocks)
    seg_last = jnp.logical_or(i == ctx_blocks - 1, i == nblk - 1)
    _fill_ext(ext_ref, cur_ref, prev_ref, next_ref, jnp.logical_not(seg_first),
              jnp.logical_not(seg_last), tb)
    pad = GDN_CONV // 2
    acc = cw_ref[0:1, :] * ext_ref[pl.ds(SUBLANES - pad, tb), :]
    for t in range(1, GDN_CONV):
        acc = acc + cw_ref[t:t + 1, :] * ext_ref[pl.ds(SUBLANES - pad + t, tb), :]
    qkv = _silu(acc)
    for h in range(GDN_HEADS):
        for j, ref in enumerate((q_ref, k_ref, v_ref)):
            sl = slice(j * BRANCH_W + h * GDN_HEAD, j * BRANCH_W + (h + 1) * GDN_HEAD)
            osl = slice(h * GDN_HEAD, (h + 1) * GDN_HEAD)
            x = qkv[:, sl]
            if j < 2:
                x = x * lax.rsqrt(jnp.sum(x * x, axis=-1, keepdims=True) + 1e-12)
            if j == 0:
                x = x * (GDN_HEAD ** -0.5)
            ref[:, osl] = x
    sc = sc_ref[...]
    lane = _iota(sc.shape, 1)
    nh2 = 2 * GDN_HEADS
    g = -jnp.exp(gp_ref[0:1, :]) * _softplus(sc + gp_ref[1:2, :])
    beta = _sigmoid(sc)
    gb_ref[...] = jnp.where(lane < nh2, g, jnp.where(lane < 2 * nh2, beta, 0.0))


def _gdn_prep(p_gdn, conv_w, gparams, n_ctx, tb):
    n = p_gdn.shape[0]
    nblk = n // tb
    qkv_cols = 3 * BRANCH_W
    r, prev_map, next_map = _halo_specs(tb, qkv_cols)
    sc_block = (4 * BRANCH_W) // LANES
    full = lambda a: pl.BlockSpec(a.shape, lambda i: (0,) * a.ndim)
    shape = jax.ShapeDtypeStruct((n, BRANCH_W), F32)
    spec = pl.BlockSpec((tb, BRANCH_W), lambda i: (i, 0))
    return pl.pallas_call(
        functools.partial(_gdn_prep_kernel, tb=tb, ctx_blocks=n_ctx // tb, nblk=nblk),
        grid=(nblk,),
        in_specs=[pl.BlockSpec((tb, qkv_cols), lambda i: (i, 0)),
                  pl.BlockSpec((SUBLANES, qkv_cols), prev_map),
                  pl.BlockSpec((SUBLANES, qkv_cols), functools.partial(next_map, nblk=nblk)),
                  pl.BlockSpec((tb, LANES), lambda i: (i, sc_block)),
                  full(conv_w), full(gparams)],
        out_specs=[spec, spec, spec, pl.BlockSpec((tb, LANES), lambda i: (i, 0))],
        out_shape=[shape, shape, shape, jax.ShapeDtypeStruct((n, LANES), F32)],
        scratch_shapes=[pltpu.VMEM((tb + 2 * SUBLANES, qkv_cols), F32)],
        compiler_params=_params(("parallel",)),
        name="gdn_prep",
    )(p_gdn, p_gdn, p_gdn, p_gdn, conv_w, gparams)


def _gdn_chunk_kernel(qf, kf, vf, gbf, qb, kb, vb, gbb, of_ref, ob_ref, s_ref):
    @pl.when(pl.program_id(0) == 0)
    def _():
        s_ref[...] = jnp.zeros(s_ref.shape, F32)

    n = CHUNK
    lane = _iota((n, LANES), 1)
    diag = _iota((LANES, LANES), 0) == _iota((LANES, LANES), 1)
    inst = []
    for z, (q_ref, k_ref, v_ref, gb_ref, o_ref) in enumerate(((qf, kf, vf, gbf, of_ref), (qb, kb, vb, gbb, ob_ref))):
        rev = bool(z)
        incl, strict = _order_masks(n, rev)
        gb = gb_ref[...]
        gam_all = _dotc(incl.astype(BF16), gb)
        gam_parts = _split3(gam_all)
        for h in range(GDN_HEADS):
            sl = slice(h * GDN_HEAD, (h + 1) * GDN_HEAD)
            j = z * GDN_HEADS + h
            gam = jnp.broadcast_to(gam_all[:, j:j + 1], (n, LANES))
            beta = jnp.broadcast_to(gb[:, 2 * GDN_HEADS + j:2 * GDN_HEADS + j + 1], (n, LANES))
            pick = (lane == j).astype(BF16)
            gam_row = sum(_dg(pick, part, NT) for part in gam_parts)
            gam_end = gam[0:1, :] if rev else gam[n - 1:n, :]
            decay = jnp.where(incl, jnp.exp(jnp.where(incl, gam[:, :n] - gam_row, 0.0)), 0.0)
            inst.append(dict(q=q_ref[:, sl], k=k_ref[:, sl], v=v_ref[:, sl], gam=gam, beta=beta, decay=decay,
                             gam_end=gam_end, strict=strict, s_ref=s_ref.at[z, h], o_ref=o_ref, sl=sl))

    for d in inst:
        d["e_gam"] = jnp.exp(d["gam"])
        kk = _dot1(d["k"], d["k"], NT)
        d["a"] = jnp.where(d["strict"], d["beta"][:, :n] * kk * d["decay"], 0.0)
        d["qk"] = _dot1(d["q"], d["k"], NT) * d["decay"]
    t = _tri_inverse([d["a"] for d in inst], _dot1)
    uw = [_dot3(ti, jnp.concatenate([d["beta"] * d["v"], d["beta"] * d["e_gam"] * d["k"]], axis=1))
          for ti, d in zip(t, inst)]
    qk_uw = [_dot1(d["qk"], x) for d, x in zip(inst, uw)]
    for d, x, qx in zip(inst, uw, qk_uw):
        s = d["s_ref"][...]
        qp = d["q"] * d["e_gam"] - qx[:, LANES:]
        d["o_ref"][:, d["sl"]] = qx[:, :LANES] + _dot3(qp, s)
        k_end = d["k"] * jnp.exp(d["gam_end"] - d["gam"])
        ke_uw = _dot1(k_end, x, TN)
        p_mat = jnp.where(diag, jnp.broadcast_to(jnp.exp(d["gam_end"]), (LANES, LANES)), 0.0) - ke_uw[:, LANES:]
        d["s_ref"][...] = _dot3(p_mat, s) + ke_uw[:, :LANES]


def _gdn_chunk(q, k, v, gb, n_ctx):
    n = q.shape[0]
    n_chunks, fwd, bwd = _chunk_maps(n, n_ctx)

    def specs(cm):
        wide = pl.BlockSpec((CHUNK, BRANCH_W), lambda i: (cm(i), 0))
        return [wide, wide, wide, pl.BlockSpec((CHUNK, LANES), lambda i: (cm(i), 0))]

    shape = jax.ShapeDtypeStruct((n, BRANCH_W), F32)
    return pl.pallas_call(
        _gdn_chunk_kernel,
        grid=(n_chunks,),
        in_specs=specs(fwd) + specs(bwd),
        out_specs=[pl.BlockSpec((CHUNK, BRANCH_W), lambda i: (fwd(i), 0)),
                   pl.BlockSpec((CHUNK, BRANCH_W), lambda i: (bwd(i), 0))],
        out_shape=[shape, shape],
        scratch_shapes=[pltpu.VMEM((2, GDN_HEADS, GDN_HEAD, GDN_HEAD), F32)],
        compiler_params=_params(("arbitrary",)),
        name="gdn_chunk",
    )(q, k, v, gb, q, k, v, gb)


def _merge_kernel(hn_ref, yrw_ref, ygla_ref, ygdn_ref, h_ref, wg_ref, wb_ref, wo_ref, mod_ref, o_ref,
                  *, n_ctx, tb):
    hn = hn_ref[...]
    merged = jnp.zeros((tb, D_MODEL), F32)
    for g, y_ref in enumerate((yrw_ref, ygla_ref, ygdn_ref)):
        gate = _sigmoid(_dg(hn, wg_ref[:, g * D_MODEL:(g + 1) * D_MODEL]))
        merged = merged + _dg(y_ref[...].astype(BF16), wb_ref[g]) * gate
    mix = _dg(merged.astype(BF16), wo_ref[...])
    gate_vec = _seg_vec(mod_ref, 2, pl.program_id(0) * tb, tb, n_ctx)
    o_ref[...] = h_ref[...] + mix * gate_vec


def _merge(hn, y_rw, y_gla, y_gdn, h, wg, wb, wo, mod, n_ctx, tb):
    n = h.shape[0]
    full = lambda a: pl.BlockSpec(a.shape, lambda i: (0,) * a.ndim)
    wide = pl.BlockSpec((tb, D_MODEL), lambda i: (i, 0))
    half = pl.BlockSpec((tb, BRANCH_W), lambda i: (i, 0))
    return pl.pallas_call(
        functools.partial(_merge_kernel, n_ctx=n_ctx, tb=tb),
        grid=(n // tb,),
        in_specs=[wide, half, half, half, wide, full(wg), full(wb), full(wo), full(mod)],
        out_specs=wide,
        out_shape=jax.ShapeDtypeStruct((n, D_MODEL), F32),
        compiler_params=_params(("parallel",)),
        name="merge_out",
    )(hn, y_rw, y_gla, y_gdn, h, wg, wb, wo, mod)


def _mlp_kernel(h_ref, g_ref, mod_ref, w1_ref, w2_ref, o_ref, hn_ref, acc_ref, *, n_ctx, tb, nj):
    j = pl.program_id(1)
    row0 = pl.program_id(0) * tb

    @pl.when(j == 0)
    def _():
        xn = _rmsnorm_rows(h_ref[...], g_ref[...])
        xn = xn * (1.0 + _seg_vec(mod_ref, 4, row0, tb, n_ctx)) + _seg_vec(mod_ref, 3, row0, tb, n_ctx)
        hn_ref[...] = xn.astype(BF16)
        acc_ref[...] = jnp.zeros(acc_ref.shape, F32)

    hid = jnp.maximum(_dg(hn_ref[...], w1_ref[...]), 0.0)
    acc_ref[...] += _dg((hid * hid).astype(BF16), w2_ref[...])

    @pl.when(j == nj - 1)
    def _():
        o_ref[...] = h_ref[...] + acc_ref[...] * _seg_vec(mod_ref, 5, row0, tb, n_ctx)


def _mlp(h, g, mod, w1, w2, n_ctx, tb, hb):
    n = h.shape[0]
    nj = MLP_HIDDEN // hb
    wide = pl.BlockSpec((tb, D_MODEL), lambda i, j: (i, 0))
    return pl.pallas_call(
        functools.partial(_mlp_kernel, n_ctx=n_ctx, tb=tb, nj=nj),
        grid=(n // tb, nj),
        in_specs=[wide, pl.BlockSpec((1, D_MODEL), lambda i, j: (0, 0)),
                  pl.BlockSpec(mod.shape, lambda i, j: (0, 0)),
                  pl.BlockSpec((D_MODEL, hb), lambda i, j: (0, j)),
                  pl.BlockSpec((hb, D_MODEL), lambda i, j: (j, 0))],
        out_specs=wide,
        out_shape=jax.ShapeDtypeStruct((n, D_MODEL), F32),
        scratch_shapes=[pltpu.VMEM((tb, D_MODEL), BF16), pltpu.VMEM((tb, D_MODEL), F32)],
        compiler_params=_params(("parallel", "arbitrary")),
        name="mlp",
    )(h, g, mod, w1, w2)


def _block_diag_ones(n, blk):
    r = jnp.arange(n) // blk
    return (r[:, None] == r[None, :]).astype(BF16)


def _pad_cols(w, cols):
    return jnp.pad(w, ((0, 0), (0, cols - w.shape[1])))


def _lora_pad(w2, lora):
    out = jnp.zeros((2, LANES, w2.shape[2]), F32)
    for z in range(2):
        out = out.at[z, z * lora:(z + 1) * lora].set(w2[z])
    return out


def _pick_block(n, n_ctx, candidates):
    for c in candidates:
        if n % c == 0 and n_ctx % c == 0:
            return c
    raise ValueError("token counts must be multiples of the chunk length")


def _pick_rows(n, candidates):
    for c in candidates:
        if n % c == 0:
            return c
    raise ValueError("unsupported token count")


def _to_colmajor(t, n_ctx, rows):
    lat = t[n_ctx:]
    lat = lat.reshape((rows, GRID_W) + lat.shape[1:]).swapaxes(0, 1).reshape(lat.shape)
    return jnp.concatenate([t[:n_ctx], lat], axis=0)


def _from_colmajor(t, n_ctx, rows):
    lat = t[n_ctx:]
    lat = lat.reshape((GRID_W, rows) + lat.shape[1:]).swapaxes(0, 1).reshape(lat.shape)
    return jnp.concatenate([t[:n_ctx], lat], axis=0)


def kernel(x, c, ctx, c_ctx, w_mod, b_mod, norm1_g, w_in, rw_mu, rw_w0, rw_w2, rw_a0, rw_a2, rw_g2, rw_kk, rw_ka, rw_rk, rw_ln_w, rw_ln_b, gla_a2, gla_ab, gla_norm_g, gdn_conv, gdn_a_log, gdn_dt_bias, gdn_norm_g, w_branch, w_out, norm2_g, w_mlp1, w_mlp2, final_g):
    bsz, n_lat, dm = x.shape
    assert bsz == 1 and dm == D_MODEL
    n_ctx = ctx.shape[1]
    n = n_ctx + n_lat
    rows = n_lat // GRID_W
    depth = w_in.shape[0]
    tb =_pick_block(n, n_ctx, (256, 128, 64))
    tm = _pick_rows(n, (640, 512, 256, 128, 64))
    tmlp = _pick_rows(n, (640, 512, 256, 128, 64))

    cc = jnp.zeros((SUBLANES, D_MODEL), F32).at[0].set(c_ctx).at[1].set(c[0])
    mod_all = _modulation(cc, w_mod, b_mod)

    bd64 = _block_diag_ones(BRANCH_W, RW_HEAD)
    bd128 = _block_diag_ones(LANES, RW_HEAD)
    h = jnp.concatenate([ctx[0], x[0]], axis=0)

    for l in range(depth):
        mod = mod_all[l]
        o0 = 0
        w_rw = w_in[l][:, o0:o0 + RW_COLS].astype(BF16)
        o0 += RW_COLS
        w_gla = _pad_cols(w_in[l][:, o0:o0 + GLA_COLS], GLA_COLS_PAD).astype(BF16)
        o0 += GLA_COLS
        w_gdn = _pad_cols(w_in[l][:, o0:o0 + GDN_COLS], GDN_COLS_PAD).astype(BF16)
        o0 += GDN_COLS
        w_gate = w_in[l][:, o0:].astype(BF16)

        hn = _norm(h, norm1_g[l][None], mod, n_ctx, tb, BF16)
        hn_cm = _to_colmajor(hn, n_ctx, rows)
        p_rw = _project(hn, w_rw, tm, "proj_rwkv")
        p_gla = _project(hn_cm, w_gla, tm, "proj_gla")
        p_gdn = _project(hn, w_gdn, tm, "proj_gdn")

        mu = rw_mu[l]
        rw_wts = {
            "mu3": jnp.zeros((SUBLANES, RW_COLS), F32).at[0].set(mu[0]).at[1].set(1 - mu[0] - mu[1]).at[2].set(mu[1]),
            "w0": rw_w0[l], "w2p": _lora_pad(rw_w2[l], RW_DECAY_LORA),
            "a0": rw_a0[l], "a2p": _lora_pad(rw_a2[l], RW_A_LORA), "g2": rw_g2[l],
            "kka": jnp.zeros((SUBLANES, BRANCH_W), F32).at[0].set(rw_kk[l]).at[1].set(rw_ka[l])
                      .at[2].set(rw_rk[l].reshape(-1)),
            "bd64": bd64,
        }
        r, v, kk, lw, kd, bdir, g, bonus = _rw_prep(p_rw, rw_wts, n_ctx, tb)
        yf, yb = _rw_chunk(r, v, kk, lw, kd, bdir, n_ctx)
        ln = jnp.zeros((SUBLANES, BRANCH_W), F32).at[0].set(rw_ln_w[l]).at[1].set(rw_ln_b[l])
        y_rw = _rw_post(yf, yb, bonus, g, ln, bd128, tb)

        of, ob = _gla_chunk(p_gla, _lora_pad(gla_a2[l], GLA_GATE_LORA), gla_ab[l], n_ctx)
        y_gla_cm = _gated_norm(of, ob, p_gla, 2, gla_norm_g[l][None], tb, "gla_post")
        y_gla = _from_colmajor(y_gla_cm, n_ctx, rows)

        conv_w = jnp.zeros((SUBLANES, 3 * BRANCH_W), F32).at[:GDN_CONV].set(gdn_conv[l])
        nh2 = 2 * GDN_HEADS
        gparams = jnp.zeros((SUBLANES, LANES), F32).at[0, :nh2].set(gdn_a_log[l].reshape(-1)) \
            .at[1, :nh2].set(gdn_dt_bias[l].reshape(-1))
        q_d, k_d, v_d, gb = _gdn_prep(p_gdn, conv_w, gparams, n_ctx, tb)
        of, ob = _gdn_chunk(q_d, k_d, v_d, gb, n_ctx)
        y_gdn = _gated_norm(of, ob, p_gdn, 3, gdn_norm_g[l][None], tb, "gdn_post")

        h = _merge(hn, y_rw, y_gla, y_gdn, h, w_gate, w_branch[l].astype(BF16), w_out[l].astype(BF16),
                   mod, n_ctx, tb)
        h = _mlp(h, norm2_g[l][None], mod, w_mlp1[l].astype(BF16), w_mlp2[l].astype(BF16), n_ctx, tmlp, 1024)

    out = _norm(h, final_g[None], mod_all[0], n_ctx, tb, F32, shift_idx=None, scale_idx=None)
    return out[n_ctx:][None]
```

```python
import functools

import jax
import jax.numpy as jnp
from jax import lax
from jax.experimental import pallas as pl
from jax.experimental.pallas import tpu as pltpu

F32 = jnp.float32
BF16 = jnp.bfloat16

D_MODEL = 1024
GRID_W = 64
CHUNK = 64
BRANCH_W = D_MODEL // 2
NORM_EPS = 1e-6
RW_HEAD = 64
RW_HEADS = BRANCH_W // RW_HEAD
RW_PAIRS = RW_HEADS // 2
RW_DECAY_LORA = 64
RW_A_LORA = 64
RW_G_LORA = 128
RW_GN_EPS = 64e-5
GLA_HEADS = 4
GLA_DK = 64
GLA_DV = BRANCH_W // GLA_HEADS
GLA_GATE_LORA = 16
GLA_GATE_NORM = 16.0
GDN_HEADS = 4
GDN_HEAD = BRANCH_W // GDN_HEADS
GDN_CONV = 5
MLP_HIDDEN = 4 * D_MODEL
RW_COLS = 3 * BRANCH_W + 2 * RW_DECAY_LORA + 2 * RW_A_LORA + RW_G_LORA
GLA_COLS = 2 * GLA_HEADS * GLA_DK + 2 * BRANCH_W + 2 * GLA_GATE_LORA
GDN_COLS = 4 * BRANCH_W + 4 * GDN_HEADS

LANES = 128
SUBLANES = 8
GLA_COLS_PAD = -(-GLA_COLS // LANES) * LANES
GDN_COLS_PAD = -(-GDN_COLS // LANES) * LANES
VMEM_LIMIT_BYTES = 48 * 1024 * 1024

RW_CPS = 2
GLA_CPS = 2
GDN_CPS = 2

NN = (((1,), (0,)), ((), ()))
NT = (((1,), (1,)), ((), ()))
TN = (((0,), (0,)), ((), ()))


def _dg(a, b, dims=NN):
    return lax.dot_general(a, b, dims, preferred_element_type=F32)


def _dot1(a, b, dims=NN):
    return _dg(a.astype(BF16), b.astype(BF16), dims)


def _split2(a):
    hi = a.astype(BF16)
    lo = (a - hi.astype(F32)).astype(BF16)
    return hi, lo


def _dot3(a, b):
    ah, al = _split2(a)
    bh, bl = _split2(b)
    m = a.shape[0]
    both = _dg(jnp.concatenate([ah, al], axis=0), bh)
    return both[:m] + (both[m:] + _dg(ah, bl))


def _split3(x):
    h1 = x.astype(BF16)
    r1 = x - h1.astype(F32)
    h2 = r1.astype(BF16)
    h3 = (r1 - h2.astype(F32)).astype(BF16)
    return h1, h2, h3


def _dotc(c_bf16, x, dims=NN):
    h1, h2, h3 = _split3(x)
    return _dg(c_bf16, h1, dims) + (_dg(c_bf16, h2, dims) + _dg(c_bf16, h3, dims))


def _dotxc(x, c_bf16, dims=NN):
    h1, h2, h3 = _split3(x)
    return _dg(h1, c_bf16, dims) + (_dg(h2, c_bf16, dims) + _dg(h3, c_bf16, dims))


def _iota(shape, axis):
    return lax.broadcasted_iota(jnp.int32, shape, axis)


def _order_masks(n, rev):
    row = _iota((n, n), 0)
    col = _iota((n, n), 1)
    if rev:
        return col >= row, col > row
    return col <= row, col < row


def _tri_inverse(a_list, dot):
    n = a_list[0].shape[0]
    row = _iota((n, n), 0)
    col = _iota((n, n), 1)
    eye = (row == col).astype(F32)
    t = [eye - jnp.where((row // 2) == (col // 2), a, 0.0) for a in a_list]
    b = 2
    while b < n:
        off = jnp.logical_and((row // (2 * b)) == (col // (2 * b)), (row // b) != (col // b))
        x = [dot(jnp.where(off, a, 0.0), ti) for a, ti in zip(a_list, t)]
        t = [ti - dot(ti, xi) for ti, xi in zip(t, x)]
        b *= 2
    return t


def _sigmoid(x):
    return 1.0 / (1.0 + jnp.exp(-x))


def _softplus(x):
    return jnp.maximum(x, 0.0) + jnp.log(1.0 + jnp.exp(-jnp.abs(x)))


def _silu(x):
    return x * _sigmoid(x)


def _params(sem):
    return pltpu.CompilerParams(dimension_semantics=sem, vmem_limit_bytes=VMEM_LIMIT_BYTES)


def _chunk_maps(n, n_ctx, cps):
    n_blocks = n // (CHUNK * cps)
    ctx_blocks = n_ctx // (CHUNK * cps)

    def fwd(i):
        return i

    def bwd(i):
        return jnp.where(i < ctx_blocks, ctx_blocks - 1 - i, n_blocks + ctx_blocks - 1 - i)

    return n_blocks, fwd, bwd


def _mod_kernel(cc_ref, w_ref, b_ref, o_ref):
    s = _silu(cc_ref[...])
    o_ref[...] = _dot3(s, w_ref[...]) + b_ref[...]


def _modulation(cc, w_mod, b_mod):
    depth = w_mod.shape[0]
    ncol = w_mod.shape[2] // D_MODEL
    return pl.pallas_call(
        _mod_kernel,
        grid=(depth, ncol),
        in_specs=[
            pl.BlockSpec((SUBLANES, D_MODEL), lambda l, j: (0, 0)),
            pl.BlockSpec((None, D_MODEL, D_MODEL), lambda l, j: (l, 0, j)),
            pl.BlockSpec((None, 1, D_MODEL), lambda l, j: (l, 0, j)),
        ],
        out_specs=pl.BlockSpec((None, SUBLANES, D_MODEL), lambda l, j: (l, 0, j)),
        out_shape=jax.ShapeDtypeStruct((depth, SUBLANES, w_mod.shape[2]), F32),
        compiler_params=_params(("parallel", "parallel")),
        name="modulation",
    )(cc, w_mod, b_mod.reshape(depth, 1, -1))


def _seg_vec(mod_ref, idx, row0, nrows, n_ctx):
    lo = idx * D_MODEL
    ctx = mod_ref[0:1, lo:lo + D_MODEL]
    lat = mod_ref[1:2, lo:lo + D_MODEL]
    is_ctx = (_iota((nrows, 1), 0) + row0) < n_ctx
    return jnp.where(is_ctx, ctx, lat)


def _rmsnorm_rows(x, g):
    ms = jnp.mean(x * x, axis=-1, keepdims=True)
    return x * lax.rsqrt(ms + NORM_EPS) * g


def _norm_kernel(x_ref, g_ref, mod_ref, o_ref, *, n_ctx, tb, shift_idx, scale_idx):
    row0 = pl.program_id(0) * tb
    xn = _rmsnorm_rows(x_ref[...], g_ref[...])
    if scale_idx is not None:
        xn = xn * (1.0 + _seg_vec(mod_ref, scale_idx, row0, tb, n_ctx)) \
            + _seg_vec(mod_ref, shift_idx, row0, tb, n_ctx)
    o_ref[...] = xn.astype(o_ref.dtype)


def _norm(x, g, mod, n_ctx, tb, out_dtype, shift_idx=0, scale_idx=1):
    n = x.shape[0]
    return pl.pallas_call(
        functools.partial(_norm_kernel, n_ctx=n_ctx, tb=tb, shift_idx=shift_idx, scale_idx=scale_idx),
        grid=(n // tb,),
        in_specs=[
            pl.BlockSpec((tb, D_MODEL), lambda i: (i, 0)),
            pl.BlockSpec((1, D_MODEL), lambda i: (0, 0)),
            pl.BlockSpec(mod.shape, lambda i: (0, 0)),
        ],
        out_specs=pl.BlockSpec((tb, D_MODEL), lambda i: (i, 0)),
        out_shape=jax.ShapeDtypeStruct((n, D_MODEL), out_dtype),
        compiler_params=_params(("parallel",)),
        name="adaln_norm",
    )(x, g, mod)


def _mm_kernel(a_ref, w_ref, o_ref):
    o_ref[...] = _dg(a_ref[...], w_ref[...])


def _project(a, w, tm, name):
    n, k = a.shape
    cols = w.shape[1]
    return pl.pallas_call(
        _mm_kernel,
        grid=(n // tm,),
        in_specs=[pl.BlockSpec((tm, k), lambda i: (i, 0)),
                  pl.BlockSpec((k, cols), lambda i: (0, 0))],
        out_specs=pl.BlockSpec((tm, cols), lambda i: (i, 0)),
        out_shape=jax.ShapeDtypeStruct((n, cols), F32),
        compiler_params=_params(("parallel",)),
        name=name,
    )(a, w)


def _fill_ext(ext_ref, cur_ref, prev_ref, next_ref, has_prev, has_next, tb):
    zeros = jnp.zeros(prev_ref.shape, F32)
    ext_ref[0:SUBLANES, :] = jnp.where(has_prev, prev_ref[...], zeros)
    ext_ref[SUBLANES:SUBLANES + tb, :] = cur_ref[...]
    ext_ref[SUBLANES + tb:SUBLANES + tb + SUBLANES, :] = jnp.where(has_next, next_ref[...], zeros)


def _halo_maps(tb, nblk):
    r = tb // SUBLANES

    def prev_map(i):
        return (jnp.maximum(i * r - 1, 0), 0)

    def next_map(i):
        return (jnp.minimum((i + 1) * r, nblk * r - 1), 0)

    return prev_map, next_map


def _rw_prep_kernel(cur_ref, prev_ref, next_ref, mu_ref, w0_ref, w2_ref, a0_ref, a2_ref, g2_ref,
                    kka_ref, bd_ref,
                    r_ref, v_ref, kk_ref, lw_ref, kd_ref, bdir_ref, g_ref, bonus_ref,
                    ext_ref, *, tb, ctx_blocks, nblk):
    i = pl.program_id(0)
    seg_first = jnp.logical_or(i == 0, i == ctx_blocks)
    seg_last = jnp.logical_or(i == ctx_blocks - 1, i == nblk - 1)
    _fill_ext(ext_ref, cur_ref, prev_ref, next_ref, jnp.logical_not(seg_first),
              jnp.logical_not(seg_last), tb)
    xs = (mu_ref[0:1, :] * ext_ref[pl.ds(SUBLANES - 1, tb), :]
          + mu_ref[1:2, :] * ext_ref[pl.ds(SUBLANES, tb), :]
          + mu_ref[2:3, :] * ext_ref[pl.ds(SUBLANES + 1, tb), :])
    bw = BRANCH_W
    r = xs[:, 0:bw]
    k = xs[:, bw:2 * bw]
    v = xs[:, 2 * bw:3 * bw]
    wl = xs[:, 3 * bw:3 * bw + LANES]
    al = xs[:, 3 * bw + LANES:3 * bw + 2 * LANES]
    gl = xs[:, 3 * bw + 2 * LANES:3 * bw + 3 * LANES]
    bd = bd_ref[...]
    k_k = kka_ref[0:1, :]
    k_a = kka_ref[1:2, :]
    r_k = kka_ref[2:3, :]

    kk0 = k * k_k
    kk = kk0 * lax.rsqrt(_dotxc(kk0 * kk0, bd) + 1e-12)
    g = _dot3(_sigmoid(gl), g2_ref[...])
    twl = jnp.tanh(wl)
    ksum = jnp.zeros_like(k)
    for z in range(2):
        wlog = w0_ref[z:z + 1, :] + _dot3(twl, w2_ref[z])
        w = -_softplus(-wlog) - 0.5
        lw = -jnp.exp(w)
        a = _sigmoid(a0_ref[z:z + 1, :] + _dot3(al, a2_ref[z]))
        kd = k * (1.0 + (a - 1.0) * k_a)
        bdir = kk * a
        ksum = ksum + kd
        for p in range(RW_PAIRS):
            sl = slice(p * LANES, (p + 1) * LANES)
            lw_ref[z, p] = lw[:, sl]
            kd_ref[z, p] = kd[:, sl]
            bdir_ref[z, p] = bdir[:, sl]
    bonus = _dotxc(r * ksum * r_k, bd) * v
    for p in range(RW_PAIRS):
        sl = slice(p * LANES, (p + 1) * LANES)
        r_ref[p] = r[:, sl]
        v_ref[p] = v[:, sl]
        kk_ref[p] = kk[:, sl]
        g_ref[p] = g[:, sl]
        bonus_ref[p] = bonus[:, sl]


def _rw_prep(p_rw, wts, n_ctx, tb):
    n = p_rw.shape[0]
    nblk = n // tb
    prev_map, next_map = _halo_maps(tb, nblk)
    full = lambda a: pl.BlockSpec(a.shape, lambda i: (0,) * a.ndim)
    pair_spec = pl.BlockSpec((RW_PAIRS, tb, LANES), lambda i: (0, i, 0))
    dir_spec = pl.BlockSpec((2, RW_PAIRS, tb, LANES), lambda i: (0, 0, i, 0))
    pair_shape = jax.ShapeDtypeStruct((RW_PAIRS, n, LANES), F32)
    dir_shape = jax.ShapeDtypeStruct((2, RW_PAIRS, n, LANES), F32)
    params = [wts["mu3"], wts["w0"], wts["w2p"], wts["a0"], wts["a2p"], wts["g2"], wts["kka"], wts["bd64"]]
    return pl.pallas_call(
        functools.partial(_rw_prep_kernel, tb=tb, ctx_blocks=n_ctx // tb, nblk=nblk),
        grid=(nblk,),
        in_specs=[pl.BlockSpec((tb, RW_COLS), lambda i: (i, 0)),
                  pl.BlockSpec((SUBLANES, RW_COLS), prev_map),
                  pl.BlockSpec((SUBLANES, RW_COLS), next_map)]
                 + [full(a) for a in params],
        out_specs=[pair_spec, pair_spec, pair_spec, dir_spec, dir_spec, dir_spec, pair_spec, pair_spec],
        out_shape=[pair_shape, pair_shape, pair_shape, dir_shape, dir_shape, dir_shape, pair_shape, pair_shape],
        scratch_shapes=[pltpu.VMEM((tb + 2 * SUBLANES, RW_COLS), F32)],
        compiler_params=_params(("parallel",)),
        name="rwkv_prep",
    )(p_rw, p_rw, p_rw, *params)


def _rw_chunk_group(items, m_ref):
    n = CHUNK
    masks = {rev: _order_masks(n, rev) for rev in (False, True)}
    head_a = _iota((n, LANES), 1) < RW_HEAD
    head_a2 = jnp.concatenate([head_a, head_a], axis=1)
    row = _iota((LANES, LANES), 0)
    col = _iota((LANES, LANES), 1)
    same_head = (row < RW_HEAD) == (col < RW_HEAD)
    diag = row == col

    for d in items:
        r, v, kk, lw, kd, bdir = d["tile"]
        incl, _ = masks[d["rev"]]
        cum = _dotc(incl.astype(BF16), lw)
        cum_end = cum[0:1, :] if d["rev"] else cum[n - 1:n, :]
        e_inv = jnp.exp(-cum)
        e_end = jnp.exp(cum_end - cum)
        d.update(v=v, kkd=kk * jnp.exp(cum - lw), kinv=kd * e_inv, binv=bdir * e_inv, rd=r * jnp.exp(cum),
                 kend=kd * e_end, bend=bdir * e_end, g_end=jnp.exp(cum_end))

    a_kb, a_kk, a_rb, a_rk = [], [], [], []
    for d in items:
        incl, strict = masks[d["rev"]]
        lhs = jnp.concatenate([jnp.where(head_a, d["kkd"], 0.0), jnp.where(head_a, 0.0, d["kkd"]),
                               jnp.where(head_a, d["rd"], 0.0), jnp.where(head_a, 0.0, d["rd"])], axis=0)
        with_b = _dot1(lhs, d["binv"], NT)
        with_k = _dot1(lhs, d["kinv"], NT)
        for h in range(2):
            a_kb.append(jnp.where(strict, with_b[h * n:(h + 1) * n], 0.0))
            a_kk.append(jnp.where(strict, with_k[h * n:(h + 1) * n], 0.0))
            a_rb.append(jnp.where(incl, with_b[(2 + h) * n:(3 + h) * n], 0.0))
            a_rk.append(jnp.where(incl, with_k[(2 + h) * n:(3 + h) * n], 0.0))
    t = _tri_inverse(a_kb, _dot1)
    inst = [(d, h) for d in items for h in range(2)]
    uw = [_dot1(t[j], jnp.concatenate([_dot1(a_kk[j], d["v"]), d["kkd"]], axis=1)) for j, (d, h) in enumerate(inst)]
    rb_uw = [_dot1(a_rb[j], uw[j]) for j in range(len(inst))]
    y0 = [_dot1(a_rk[j], d["v"]) - rb_uw[j][:, :LANES] for j, (d, h) in enumerate(inst)]
    qp = [d["rd"] - rb_uw[j][:, LANES:] for j, (d, h) in enumerate(inst)]
    for i, d in enumerate(items):
        ja, jb = 2 * i, 2 * i + 1
        uw_i = jnp.where(head_a2, uw[ja], uw[jb])
        d["y0"] = jnp.where(head_a, y0[ja], y0[jb])
        d["qp"] = jnp.where(head_a, qp[ja], qp[jb])
        b_uw = _dot1(d["bend"], uw_i, TN)
        p_mat = jnp.where(diag, jnp.broadcast_to(d["g_end"], (LANES, LANES)), 0.0) - b_uw[:, LANES:]
        d["p_mat"] = jnp.where(same_head, p_mat, 0.0)
        d["z0"] = _dot1(d["kend"], d["v"], TN) - b_uw[:, :LANES]

    for z, p in sorted({(d["z"], d["p"]) for d in items}):
        mine = sorted((d for d in items if (d["z"], d["p"]) == (z, p)), key=lambda d: d["c"], reverse=bool(z))
        m = m_ref[z, p]
        for d in mine:
            qm_pm = _dot3(jnp.concatenate([d["qp"], d["p_mat"]], axis=0), m)
            d["y_ref"][p, d["rows"], :] = d["y0"] + qm_pm[:n]
            m = jnp.where(same_head, qm_pm[n:] + d["z0"], 0.0)
        m_ref[z, p] = m


def _rw_chunk_kernel(rf, vf, kkf, lwf, kdf, bf, rb, vb, kkb, lwb, kdb, bb, yf_ref, yb_ref, m_ref):
    @pl.when(pl.program_id(0) == 0)
    def _():
        m_ref[...] = jnp.zeros(m_ref.shape, F32)

    dirs = ((rf, vf, kkf, lwf, kdf, bf, yf_ref, False), (rb, vb, kkb, lwb, kdb, bb, yb_ref, True))
    items = []
    for z, (r, v, kk, lw, kd, b, y_ref, rev) in enumerate(dirs):
        for p in range(RW_PAIRS):
            for c in range(RW_CPS):
                rows = slice(c * CHUNK, (c + 1) * CHUNK)
                items.append(dict(z=z, p=p, c=c, rev=rev, rows=rows, y_ref=y_ref,
                                  tile=tuple(x[p, rows, :] for x in (r, v, kk, lw, kd, b))))
    _rw_chunk_group(items, m_ref)


def _rw_chunk(r, v, kk, lw, kd, bdir, n_ctx):
    n = r.shape[1]
    n_blocks, fwd, bwd = _chunk_maps(n, n_ctx, RW_CPS)
    rows = CHUNK * RW_CPS

    def shared(cm):
        return pl.BlockSpec((RW_PAIRS, rows, LANES), lambda i: (0, cm(i), 0))

    def per_dir(z, cm):
        return pl.BlockSpec((None, RW_PAIRS, rows, LANES), lambda i: (z, 0, cm(i), 0))

    in_specs = [shared(fwd), shared(fwd), shared(fwd), per_dir(0, fwd), per_dir(0, fwd), per_dir(0, fwd),
                shared(bwd), shared(bwd), shared(bwd), per_dir(1, bwd), per_dir(1, bwd), per_dir(1, bwd)]
    shape = jax.ShapeDtypeStruct((RW_PAIRS, n, LANES), F32)
    return pl.pallas_call(
        _rw_chunk_kernel,
        grid=(n_blocks,),
        in_specs=in_specs,
        out_specs=[shared(fwd), shared(bwd)],
        out_shape=[shape, shape],
        scratch_shapes=[pltpu.VMEM((2, RW_PAIRS, LANES, LANES), F32)],
        compiler_params=_params(("arbitrary",)),
        name="rwkv_chunk",
    )(r, v, kk, lw, kd, bdir, r, v, kk, lw, kd, bdir)


def _rw_post_kernel(yf_ref, yb_ref, bonus_ref, g_ref, ln_ref, bd_ref, o_ref):
    bd = bd_ref[...]
    inv_n = 1.0 / RW_HEAD
    for p in range(RW_PAIRS):
        sl = slice(p * LANES, (p + 1) * LANES)
        y = yf_ref[p] + yb_ref[p]
        mu = _dotxc(y, bd) * inv_n
        d = y - mu
        var = _dotxc(d * d, bd) * inv_n
        yn = d * lax.rsqrt(var + RW_GN_EPS) * ln_ref[0:1, sl] + ln_ref[1:2, sl]
        o_ref[:, sl] = (yn + bonus_ref[p]) * g_ref[p]


def _rw_post(yf, yb, bonus, g, ln, bd128, tb):
    n = yf.shape[1]
    pair_spec = pl.BlockSpec((RW_PAIRS, tb, LANES), lambda i: (0, i, 0))
    full = lambda a: pl.BlockSpec(a.shape, lambda i: (0,) * a.ndim)
    return pl.pallas_call(
        _rw_post_kernel,
        grid=(n // tb,),
        in_specs=[pair_spec, pair_spec, pair_spec, pair_spec, full(ln), full(bd128)],
        out_specs=pl.BlockSpec((tb, BRANCH_W), lambda i: (i, 0)),
        out_shape=jax.ShapeDtypeStruct((n, BRANCH_W), F32),
        compiler_params=_params(("parallel",)),
        name="rwkv_post",
    )(yf, yb, bonus, g, ln, bd128)


def _gla_chunk_kernel(qf, kf, vf, alf, qb, kb, vb, alb, a2_ref, ab_ref, of_ref, ob_ref, s_ref):
    @pl.when(pl.program_id(0) == 0)
    def _():
        s_ref[...] = jnp.zeros(s_ref.shape, F32)

    n = CHUNK
    head_a = _iota((n, LANES), 1) < GLA_DK
    top = _iota((LANES, LANES), 0) < GLA_DK
    items = []
    for z, (q_ref, k_ref, v_ref, al_ref, o_ref) in enumerate(((qf, kf, vf, alf, of_ref), (qb, kb, vb, alb, ob_ref))):
        rev = bool(z)
        incl, _ = _order_masks(n, rev)
        for c in range(GLA_CPS):
            rows = slice(c * n, (c + 1) * n)
            zlog = _dot3(al_ref[rows, :], a2_ref[z]) + ab_ref[z:z + 1, :]
            log_a = (jnp.minimum(zlog, 0.0) - jnp.log(1.0 + jnp.exp(-jnp.abs(zlog)))) * (1.0 / GLA_GATE_NORM)
            bcum = _dotc(incl.astype(BF16), log_a)
            b_end = bcum[0:1, :] if rev else bcum[n - 1:n, :]
            q_dec = q_ref[rows, :] * (GLA_DK ** -0.5) * jnp.exp(bcum)
            k_inv = k_ref[rows, :] * jnp.exp(-bcum)
            k_end = k_ref[rows, :] * jnp.exp(b_end - bcum)
            dec_end = jnp.exp(b_end)
            for p in range(GLA_HEADS // 2):
                sl = slice(p * LANES, (p + 1) * LANES)
                items.append(dict(z=z, c=c, p=p, incl=incl, rows=rows, o_ref=o_ref, qd=q_dec[:, sl], ki=k_inv[:, sl],
                                  ke=k_end[:, sl], dec=dec_end[:, sl],
                                  v=[v_ref[rows, (2 * p + h) * GLA_DV:(2 * p + h + 1) * GLA_DV] for h in range(2)]))
    for d in items:
        d["qd_h"] = [jnp.where(head_a if h == 0 else jnp.logical_not(head_a), d["qd"], 0.0) for h in range(2)]
        d["att"] = [jnp.where(d["incl"], _dot1(d["qd_h"][h], d["ki"], NT), 0.0) for h in range(2)]
    for d in items:
        d["intra"] = [_dot1(d["att"][h], d["v"][h]) for h in range(2)]
        upd = [_dot1(d["ke"], d["v"][h], TN) for h in range(2)]
        d["upd"] = jnp.where(top, upd[0], upd[1])
        d["dec_col"] = jnp.transpose(jnp.broadcast_to(d["dec"], (LANES, LANES)))
    for z in range(2):
        order = range(GLA_CPS - 1, -1, -1) if z else range(GLA_CPS)
        for p in range(GLA_HEADS // 2):
            s = s_ref[z, p]
            for c in order:
                d = next(x for x in items if (x["z"], x["c"], x["p"]) == (z, c, p))
                inter = _dot3(jnp.concatenate(d["qd_h"], axis=0), s)
                for h in range(2):
                    hh = 2 * p + h
                    d["o_ref"][d["rows"], hh * GLA_DV:(hh + 1) * GLA_DV] = d["intra"][h] + inter[h * n:(h + 1) * n]
                s = d["dec_col"] * s + d["upd"]
            s_ref[z, p] = s


def _gla_chunk(p_gla, a2p, ab, n_ctx):
    n = p_gla.shape[0]
    n_blocks, fwd, bwd = _chunk_maps(n, n_ctx, GLA_CPS)
    hk = GLA_HEADS * GLA_DK
    rows = CHUNK * GLA_CPS

    def specs(cm):
        return [pl.BlockSpec((rows, hk), lambda i: (cm(i), 0)),
                pl.BlockSpec((rows, hk), lambda i: (cm(i), 1)),
                pl.BlockSpec((rows, BRANCH_W), lambda i: (cm(i), 1)),
                pl.BlockSpec((rows, LANES), lambda i: (cm(i), (2 * hk + 2 * BRANCH_W) // LANES))]

    full = lambda a: pl.BlockSpec(a.shape, lambda i: (0,) * a.ndim)
    shape = jax.ShapeDtypeStruct((n, BRANCH_W), F32)
    return pl.pallas_call(
        _gla_chunk_kernel,
        grid=(n_blocks,),
        in_specs=specs(fwd) + specs(bwd) + [full(a2p), full(ab)],
        out_specs=[pl.BlockSpec((rows, BRANCH_W), lambda i: (fwd(i), 0)),
                   pl.BlockSpec((rows, BRANCH_W), lambda i: (bwd(i), 0))],
        out_shape=[shape, shape],
        scratch_shapes=[pltpu.VMEM((2, GLA_HEADS // 2, LANES, LANES), F32)],
        compiler_params=_params(("arbitrary",)),
        name="gla_chunk",
    )(p_gla, p_gla, p_gla, p_gla, p_gla, p_gla, p_gla, p_gla, a2p, ab)


def _gated_norm_kernel(of_ref, ob_ref, gate_ref, g_ref, o_ref):
    for h in range(BRANCH_W // LANES):
        sl = slice(h * LANES, (h + 1) * LANES)
        o = of_ref[:, sl] + ob_ref[:, sl]
        ms = jnp.mean(o * o, axis=-1, keepdims=True)
        o_ref[:, sl] = o * lax.rsqrt(ms + NORM_EPS) * g_ref[...] * _silu(gate_ref[:, sl])


def _gated_norm(of, ob, gate_src, gate_block, norm_g, tb, name):
    n = of.shape[0]
    spec = pl.BlockSpec((tb, BRANCH_W), lambda i: (i, 0))
    return pl.pallas_call(
        _gated_norm_kernel,
        grid=(n // tb,),
        in_specs=[spec, spec, pl.BlockSpec((tb, BRANCH_W), lambda i: (i, gate_block)),
                  pl.BlockSpec((1, LANES), lambda i: (0, 0))],
        out_specs=spec,
        out_shape=jax.ShapeDtypeStruct((n, BRANCH_W), F32),
        compiler_params=_params(("parallel",)),
        name=name,
    )(of, ob, gate_src, norm_g)


def _gdn_prep_kernel(cur_ref, prev_ref, next_ref, sc_ref, cw_ref, gp_ref,
                     q_ref, k_ref, v_ref, gb_ref, ext_ref, *, tb, ctx_blocks, nblk):
    i = pl.program_id(0)
    seg_first = jnp.logical_or(i == 0, i == ctx_blocks)
    seg_last = jnp.logical_or(i == ctx_blocks - 1, i == nblk - 1)
    _fill_ext(ext_ref, cur_ref, prev_ref, next_ref, jnp.logical_not(seg_first),
              jnp.logical_not(seg_last), tb)
    pad = GDN_CONV // 2
    acc = cw_ref[0:1, :] * ext_ref[pl.ds(SUBLANES - pad, tb), :]
    for t in range(1, GDN_CONV):
        acc = acc + cw_ref[t:t + 1, :] * ext_ref[pl.ds(SUBLANES - pad + t, tb), :]
    qkv = _silu(acc)
    for h in range(GDN_HEADS):
        for j, ref in enumerate((q_ref, k_ref, v_ref)):
            sl = slice(j * BRANCH_W + h * GDN_HEAD, j * BRANCH_W + (h + 1) * GDN_HEAD)
            osl = slice(h * GDN_HEAD, (h + 1) * GDN_HEAD)
            x = qkv[:, sl]
            if j < 2:
                x = x * lax.rsqrt(jnp.sum(x * x, axis=-1, keepdims=True) + 1e-12)
            if j == 0:
                x = x * (GDN_HEAD ** -0.5)
            ref[:, osl] = x
    sc = sc_ref[...]
    lane = _iota(sc.shape, 1)
    nh2 = 2 * GDN_HEADS
    g = -jnp.exp(gp_ref[0:1, :]) * _softplus(sc + gp_ref[1:2, :])
    beta = _sigmoid(sc)
    gb_ref[...] = jnp.where(lane < nh2, g, jnp.where(lane < 2 * nh2, beta, 0.0))


def _gdn_prep(p_gdn, conv_w, gparams, n_ctx, tb):
    n = p_gdn.shape[0]
    nblk = n // tb
    qkv_cols = 3 * BRANCH_W
    prev_map, next_map = _halo_maps(tb, nblk)
    sc_block = (4 * BRANCH_W) // LANES
    full = lambda a: pl.BlockSpec(a.shape, lambda i: (0,) * a.ndim)
    shape = jax.ShapeDtypeStruct((n, BRANCH_W), F32)
    spec = pl.BlockSpec((tb, BRANCH_W), lambda i: (i, 0))
    return pl.pallas_call(
        functools.partial(_gdn_prep_kernel, tb=tb, ctx_blocks=n_ctx // tb, nblk=nblk),
        grid=(nblk,),
        in_specs=[pl.BlockSpec((tb, qkv_cols), lambda i: (i, 0)),
                  pl.BlockSpec((SUBLANES, qkv_cols), prev_map),
                  pl.BlockSpec((SUBLANES, qkv_cols), next_map),
                  pl.BlockSpec((tb, LANES), lambda i: (i, sc_block)),
                  full(conv_w), full(gparams)],
        out_specs=[spec, spec, spec, pl.BlockSpec((tb, LANES), lambda i: (i, 0))],
        out_shape=[shape, shape, shape, jax.ShapeDtypeStruct((n, LANES), F32)],
        scratch_shapes=[pltpu.VMEM((tb + 2 * SUBLANES, qkv_cols), F32)],
        compiler_params=_params(("parallel",)),
        name="gdn_prep",
    )(p_gdn, p_gdn, p_gdn, p_gdn, conv_w, gparams)


def _gdn_chunk_kernel(qf, kf, vf, gbf, qb, kb, vb, gbb, of_ref, ob_ref, s_ref):
    @pl.when(pl.program_id(0) == 0)
    def _():
        s_ref[...] = jnp.zeros(s_ref.shape, F32)

    n = CHUNK
    lane = _iota((n, LANES), 1)
    diag = _iota((LANES, LANES), 0) == _iota((LANES, LANES), 1)
    inst = []
    for z, (q_ref, k_ref, v_ref, gb_ref, o_ref) in enumerate(((qf, kf, vf, gbf, of_ref), (qb, kb, vb, gbb, ob_ref))):
        rev = bool(z)
        incl, strict = _order_masks(n, rev)
        for c in range(GDN_CPS):
            rows = slice(c * n, (c + 1) * n)
            gb = gb_ref[rows, :]
            gam_all = _dotc(incl.astype(BF16), gb)
            gam_parts = _split3(gam_all)
            for h in range(GDN_HEADS):
                sl = slice(h * GDN_HEAD, (h + 1) * GDN_HEAD)
                j = z * GDN_HEADS + h
                gam = jnp.broadcast_to(gam_all[:, j:j + 1], (n, LANES))
                beta = jnp.broadcast_to(gb[:, 2 * GDN_HEADS + j:2 * GDN_HEADS + j + 1], (n, LANES))
                pick = (lane == j).astype(BF16)
                gam_row = sum(_dg(pick, part, NT) for part in gam_parts)
                gam_end = gam[0:1, :] if rev else gam[n - 1:n, :]
                decay = jnp.where(incl, jnp.exp(jnp.where(incl, gam[:, :n] - gam_row, 0.0)), 0.0)
                inst.append(dict(q=q_ref[rows, sl], k=k_ref[rows, sl], v=v_ref[rows, sl], gam=gam, beta=beta,
                                 decay=decay, gam_end=gam_end, strict=strict, z=z, h=h, c=c, o_ref=o_ref,
                                 rows=rows, sl=sl))

    for d in inst:
        d["e_gam"] = jnp.exp(d["gam"])
        kq = _dot1(jnp.concatenate([d["k"], d["q"]], axis=0), d["k"], NT)
        d["a"] = jnp.where(d["strict"], d["beta"][:, :n] * kq[:n] * d["decay"], 0.0)
        d["qk"] = kq[n:] * d["decay"]
    t = _tri_inverse([d["a"] for d in inst], _dot1)
    for d, ti in zip(inst, t):
        d["uw"] = _dot1(ti, jnp.concatenate([d["beta"] * d["v"], d["beta"] * d["e_gam"] * d["k"]], axis=1))
    for d in inst:
        qx = _dot1(d["qk"], d["uw"])
        d["y0"] = qx[:, :LANES]
        d["qp"] = d["q"] * d["e_gam"] - qx[:, LANES:]
        k_end = d["k"] * jnp.exp(d["gam_end"] - d["gam"])
        ke_uw = _dot1(k_end, d["uw"], TN)
        d["p"] = jnp.where(diag, jnp.broadcast_to(jnp.exp(d["gam_end"]), (LANES, LANES)), 0.0) - ke_uw[:, LANES:]
        d["z0"] = ke_uw[:, :LANES]
    for z in range(2):
        order = range(GDN_CPS - 1, -1, -1) if z else range(GDN_CPS)
        for h in range(GDN_HEADS):
            s = s_ref[z, h]
            for c in order:
                d = next(x for x in inst if (x["z"], x["h"], x["c"]) == (z, h, c))
                qs_ps = _dot3(jnp.concatenate([d["qp"], d["p"]], axis=0), s)
                d["o_ref"][d["rows"], d["sl"]] = d["y0"] + qs_ps[:n]
                s = qs_ps[n:] + d["z0"]
            s_ref[z, h] = s


def _gdn_chunk(q, k, v, gb, n_ctx):
    n = q.shape[0]
    n_blocks, fwd, bwd = _chunk_maps(n, n_ctx, GDN_CPS)
    rows = CHUNK * GDN_CPS

    def specs(cm):
        wide = pl.BlockSpec((rows, BRANCH_W), lambda i: (cm(i), 0))
        return [wide, wide, wide, pl.BlockSpec((rows, LANES), lambda i: (cm(i), 0))]

    shape = jax.ShapeDtypeStruct((n, BRANCH_W), F32)
    return pl.pallas_call(
        _gdn_chunk_kernel,
        grid=(n_blocks,),
        in_specs=specs(fwd) + specs(bwd),
        out_specs=[pl.BlockSpec((rows, BRANCH_W), lambda i: (fwd(i), 0)),
                   pl.BlockSpec((rows, BRANCH_W), lambda i: (bwd(i), 0))],
        out_shape=[shape, shape],
        scratch_shapes=[pltpu.VMEM((2, GDN_HEADS, GDN_HEAD, GDN_HEAD), F32)],
        compiler_params=_params(("arbitrary",)),
        name="gdn_chunk",
    )(q, k, v, gb, q, k, v, gb)


def _merge_kernel(hn_ref, yrw_ref, ygla_ref, ygdn_ref, h_ref, wg_ref, wb_ref, wo_ref, mod_ref, o_ref,
                  *, n_ctx, tb):
    hn = hn_ref[...]
    merged = jnp.zeros((tb, D_MODEL), F32)
    for g, y_ref in enumerate((yrw_ref, ygla_ref, ygdn_ref)):
        gate = _sigmoid(_dg(hn, wg_ref[:, g * D_MODEL:(g + 1) * D_MODEL]))
        merged = merged + _dg(y_ref[...].astype(BF16), wb_ref[g]) * gate
    mix = _dg(merged.astype(BF16), wo_ref[...])
    gate_vec = _seg_vec(mod_ref, 2, pl.program_id(0) * tb, tb, n_ctx)
    o_ref[...] = h_ref[...] + mix * gate_vec


def _merge(hn, y_rw, y_gla, y_gdn, h, wg, wb, wo, mod, n_ctx, tb):
    n = h.shape[0]
    full = lambda a: pl.BlockSpec(a.shape, lambda i: (0,) * a.ndim)
    wide = pl.BlockSpec((tb, D_MODEL), lambda i: (i, 0))
    half = pl.BlockSpec((tb, BRANCH_W), lambda i: (i, 0))
    return pl.pallas_call(
        functools.partial(_merge_kernel, n_ctx=n_ctx, tb=tb),
        grid=(n // tb,),
        in_specs=[wide, half, half, half, wide, full(wg), full(wb), full(wo), full(mod)],
        out_specs=wide,
        out_shape=jax.ShapeDtypeStruct((n, D_MODEL), F32),
        compiler_params=_params(("parallel",)),
        name="merge_out",
    )(hn, y_rw, y_gla, y_gdn, h, wg, wb, wo, mod)


def _mlp_kernel(h_ref, g_ref, mod_ref, w1_ref, w2_ref, o_ref, hn_ref, acc_ref, *, n_ctx, tb, nj):
    j = pl.program_id(1)
    row0 = pl.program_id(0) * tb

    @pl.when(j == 0)
    def _():
        xn = _rmsnorm_rows(h_ref[...], g_ref[...])
        xn = xn * (1.0 + _seg_vec(mod_ref, 4, row0, tb, n_ctx)) + _seg_vec(mod_ref, 3, row0, tb, n_ctx)
        hn_ref[...] = xn.astype(BF16)
        acc_ref[...] = jnp.zeros(acc_ref.shape, F32)

    hid = jnp.maximum(_dg(hn_ref[...], w1_ref[...]), 0.0)
    acc_ref[...] += _dg((hid * hid).astype(BF16), w2_ref[...])

    @pl.when(j == nj - 1)
    def _():
        o_ref[...] = h_ref[...] + acc_ref[...] * _seg_vec(mod_ref, 5, row0, tb, n_ctx)


def _mlp(h, g, mod, w1, w2, n_ctx, tb, hb):
    n = h.shape[0]
    nj = MLP_HIDDEN // hb
    wide = pl.BlockSpec((tb, D_MODEL), lambda i, j: (i, 0))
    return pl.pallas_call(
        functools.partial(_mlp_kernel, n_ctx=n_ctx, tb=tb, nj=nj),
        grid=(n // tb, nj),
        in_specs=[wide, pl.BlockSpec((1, D_MODEL), lambda i, j: (0, 0)),
                  pl.BlockSpec(mod.shape, lambda i, j: (0, 0)),
                  pl.BlockSpec((D_MODEL, hb), lambda i, j: (0, j)),
                  pl.BlockSpec((hb, D_MODEL), lambda i, j: (j, 0))],
        out_specs=wide,
        out_shape=jax.ShapeDtypeStruct((n, D_MODEL), F32),
        scratch_shapes=[pltpu.VMEM((tb, D_MODEL), BF16), pltpu.VMEM((tb, D_MODEL), F32)],
        compiler_params=_params(("parallel", "arbitrary")),
        name="mlp",
    )(h, g, mod, w1, w2)


def _block_diag_ones(n, blk):
    r = jnp.arange(n) // blk
    return (r[:, None] == r[None, :]).astype(BF16)


def _pad_cols(w, cols):
    return jnp.pad(w, ((0, 0), (0, cols - w.shape[1])))


def _lora_pad(w2, lora):
    out = jnp.zeros((2, LANES, w2.shape[2]), F32)
    for z in range(2):
        out = out.at[z, z * lora:(z + 1) * lora].set(w2[z])
    return out


def _pick_block(n, n_ctx, candidates):
    for c in candidates:
        if n % c == 0 and n_ctx % c == 0:
            return c
    raise ValueError("token counts must be multiples of the chunk length")


def _pick_rows(n, candidates):
    for c in candidates:
        if n % c == 0:
            return c
    raise ValueError("unsupported token count")


def _to_colmajor(t, n_ctx, rows):
    lat = t[n_ctx:]
    lat = lat.reshape((rows, GRID_W) + lat.shape[1:]).swapaxes(0, 1).reshape(lat.shape)
    return jnp.concatenate([t[:n_ctx], lat], axis=0)


def _from_colmajor(t, n_ctx, rows):
    lat = t[n_ctx:]
    lat = lat.reshape((GRID_W, rows) + lat.shape[1:]).swapaxes(0, 1).reshape(lat.shape)
    return jnp.concatenate([t[:n_ctx], lat], axis=0)


def kernel(x, c, ctx, c_ctx, w_mod, b_mod, norm1_g, w_in, rw_mu, rw_w0, rw_w2, rw_a0, rw_a2, rw_g2, rw_kk, rw_ka, rw_rk, rw_ln_w, rw_ln_b, gla_a2, gla_ab, gla_norm_g, gdn_conv, gdn_a_log, gdn_dt_bias, gdn_norm_g, w_branch, w_out, norm2_g, w_mlp1, w_mlp2, final_g):
    bsz, n_lat, dm = x.shape
    assert bsz == 1 and dm == D_MODEL
    n_ctx = ctx.shape[1]
    n = n_ctx + n_lat
    rows = n_lat // GRID_W
    depth = w_in.shape[0]
    tb = _pick_block(n, n_ctx, (256, 128))
    tm = _pick_rows(n, (640, 512, 256, 128))
    tmlp = _pick_rows(n, (640, 512, 256, 128))

    cc = jnp.zeros((SUBLANES, D_MODEL), F32).at[0].set(c_ctx).at[1].set(c[0])
    mod_all = _modulation(cc, w_mod, b_mod)

    bd64 = _block_diag_ones(BRANCH_W, RW_HEAD)
    bd128 = _block_diag_ones(LANES, RW_HEAD)
    h = jnp.concatenate([ctx[0], x[0]], axis=0)

    for l in range(depth):
        mod = mod_all[l]
        o0 = 0
        w_rw = w_in[l][:, o0:o0 + RW_COLS].astype(BF16)
        o0 += RW_COLS
        w_gla = _pad_cols(w_in[l][:, o0:o0 + GLA_COLS], GLA_COLS_PAD).astype(BF16)
        o0 += GLA_COLS
        w_gdn = _pad_cols(w_in[l][:, o0:o0 + GDN_COLS], GDN_COLS_PAD).astype(BF16)
        o0 += GDN_COLS
        w_gate = w_in[l][:, o0:].astype(BF16)

        hn = _norm(h, norm1_g[l][None], mod, n_ctx, tb, BF16)
        hn_cm = _to_colmajor(hn, n_ctx, rows)
        p_rw = _project(hn, w_rw, tm, "proj_rwkv")
        p_gla = _project(hn_cm, w_gla, tm, "proj_gla")
        p_gdn = _project(hn, w_gdn, tm, "proj_gdn")

        mu = rw_mu[l]
        rw_wts = {
            "mu3": jnp.zeros((SUBLANES, RW_COLS), F32).at[0].set(mu[0]).at[1].set(1 - mu[0] - mu[1]).at[2].set(mu[1]),
            "w0": rw_w0[l], "w2p": _lora_pad(rw_w2[l], RW_DECAY_LORA),
            "a0": rw_a0[l], "a2p": _lora_pad(rw_a2[l], RW_A_LORA), "g2": rw_g2[l],
            "kka": jnp.zeros((SUBLANES, BRANCH_W), F32).at[0].set(rw_kk[l]).at[1].set(rw_ka[l])
                      .at[2].set(rw_rk[l].reshape(-1)),
            "bd64": bd64,
        }
        r, v, kk, lw, kd, bdir, g, bonus = _rw_prep(p_rw, rw_wts, n_ctx, tb)
        yf, yb = _rw_chunk(r, v, kk, lw, kd, bdir, n_ctx)
        ln = jnp.zeros((SUBLANES, BRANCH_W), F32).at[0].set(rw_ln_w[l]).at[1].set(rw_ln_b[l])
        y_rw = _rw_post(yf, yb, bonus, g, ln, bd128, tb)

        of, ob = _gla_chunk(p_gla, _lora_pad(gla_a2[l], GLA_GATE_LORA), gla_ab[l], n_ctx)
        y_gla_cm = _gated_norm(of, ob, p_gla, 2, gla_norm_g[l][None], tb, "gla_post")
        y_gla = _from_colmajor(y_gla_cm, n_ctx, rows)

        conv_w = jnp.zeros((SUBLANES, 3 * BRANCH_W), F32).at[:GDN_CONV].set(gdn_conv[l])
        nh2 = 2 * GDN_HEADS
        gparams = jnp.zeros((SUBLANES, LANES), F32).at[0, :nh2].set(gdn_a_log[l].reshape(-1)) \
            .at[1, :nh2].set(gdn_dt_bias[l].reshape(-1))
        q_d, k_d, v_d, gb = _gdn_prep(p_gdn, conv_w, gparams, n_ctx, tb)
        of, ob = _gdn_chunk(q_d, k_d, v_d, gb, n_ctx)
        y_gdn = _gated_norm(of, ob, p_gdn, 3, gdn_norm_g[l][None], tb, "gdn_post")

        h = _merge(hn, y_rw, y_gla, y_gdn, h, w_gate, w_branch[l].astype(BF16), w_out[l].astype(BF16),
                   mod, n_ctx, tb)
        h = _mlp(h, norm2_g[l][None], mod, w_mlp1[l].astype(BF16), w_mlp2[l].astype(BF16), n_ctx, tmlp, 1024)

    out = _norm(h, final_g[None], mod_all[0], n_ctx, tb, F32, shift_idx=None, scale_idx=None)
    return out[n_ctx:][None]
```

```python
import functools

import jax
import jax.numpy as jnp
from jax import lax
from jax.experimental import pallas as pl
from jax.experimental.pallas import tpu as pltpu

F32 = jnp.float32
BF16 = jnp.bfloat16

D_MODEL = 1024
GRID_W = 64
CHUNK = 64
BRANCH_W = D_MODEL // 2
NORM_EPS = 1e-6
RW_HEAD = 64
RW_HEADS = BRANCH_W // RW_HEAD
RW_PAIRS = RW_HEADS // 2
RW_DECAY_LORA = 64
RW_A_LORA = 64
RW_G_LORA = 128
RW_GN_EPS = 64e-5
GLA_HEADS = 4
GLA_DK = 64
GLA_DV = BRANCH_W // GLA_HEADS
GLA_GATE_LORA = 16
GLA_GATE_NORM = 16.0
GDN_HEADS = 4
GDN_HEAD = BRANCH_W // GDN_HEADS
GDN_CONV = 5
MLP_HIDDEN = 4 * D_MODEL
RW_COLS = 3 * BRANCH_W + 2 * RW_DECAY_LORA + 2 * RW_A_LORA + RW_G_LORA
GLA_COLS = 2 * GLA_HEADS * GLA_DK + 2 * BRANCH_W + 2 * GLA_GATE_LORA
GDN_COLS = 4 * BRANCH_W + 4 * GDN_HEADS

LANES = 128
SUBLANES = 8
GLA_COLS_PAD = -(-GLA_COLS // LANES) * LANES
GDN_COLS_PAD = -(-GDN_COLS // LANES) * LANES
VMEM_LIMIT_BYTES = 48 * 1024 * 1024

RW_CPS = 2
GLA_CPS = 2
GDN_CPS = 2

NN = (((1,), (0,)), ((), ()))
NT = (((1,), (1,)), ((), ()))
TN = (((0,), (0,)), ((), ()))


def _dg(a, b, dims=NN):
    return lax.dot_general(a, b, dims, preferred_element_type=F32)


def _dot1(a, b, dims=NN):
    return _dg(a.astype(BF16), b.astype(BF16), dims)


def _split2(a):
    hi = a.astype(BF16)
    lo = (a - hi.astype(F32)).astype(BF16)
    return hi, lo


def _dot3(a, b):
    ah, al = _split2(a)
    bh, bl = _split2(b)
    m = a.shape[0]
    both = _dg(jnp.concatenate([ah, al], axis=0), bh)
    return both[:m] + (both[m:] + _dg(ah, bl))


def _split3(x):
    h1 = x.astype(BF16)
    r1 = x - h1.astype(F32)
    h2 = r1.astype(BF16)
    h3 = (r1 - h2.astype(F32)).astype(BF16)
    return h1, h2, h3


def _dotc(c_bf16, x, dims=NN):
    h1, h2, h3 = _split3(x)
    return _dg(c_bf16, h1, dims) + (_dg(c_bf16, h2, dims) + _dg(c_bf16, h3, dims))


def _dotxc(x, c_bf16, dims=NN):
    h1, h2, h3 = _split3(x)
    return _dg(h1, c_bf16, dims) + (_dg(h2, c_bf16, dims) + _dg(h3, c_bf16, dims))


def _iota(shape, axis):
    return lax.broadcasted_iota(jnp.int32, shape, axis)


def _order_masks(n, rev):
    row = _iota((n, n), 0)
    col = _iota((n, n), 1)
    if rev:
        return col >= row, col > row
    return col <= row, col < row


def _pair_rows(x, first):
    return jnp.concatenate([jnp.where(first, x, 0.0), jnp.where(first, 0.0, x)], axis=0)


def _tri_inverse_pairs(a_list):
    n = a_list[0].shape[0]
    row = _iota((n, LANES), 0)
    col = _iota((n, LANES), 1) % n
    r2 = _iota((LANES, LANES), 0)
    c2 = _iota((LANES, LANES), 1)
    same_head = (r2 < n) == (c2 < n)

    def bdiag(m):
        return jnp.where(same_head, jnp.concatenate([m, m], axis=0), 0.0)

    eye = (row == col).astype(F32)
    t = [eye - jnp.where((row // 2) == (col // 2), a, 0.0) for a in a_list]
    b = 2
    while b < n:
        off = jnp.logical_and((row // (2 * b)) == (col // (2 * b)), (row // b) != (col // b))
        x = [_dot1(jnp.where(off, a, 0.0), bdiag(ti)) for a, ti in zip(a_list, t)]
        t = [ti - _dot1(ti, bdiag(xi)) for ti, xi in zip(t, x)]
        b *= 2
    return t


def _sigmoid(x):
    return 1.0 / (1.0 + jnp.exp(-x))


def _softplus(x):
    return jnp.maximum(x, 0.0) + jnp.log(1.0 + jnp.exp(-jnp.abs(x)))


def _silu(x):
    return x * _sigmoid(x)


def _params(sem):
    return pltpu.CompilerParams(dimension_semantics=sem, vmem_limit_bytes=VMEM_LIMIT_BYTES)


def _chunk_maps(n, n_ctx, cps):
    n_blocks = n // (CHUNK * cps)
    ctx_blocks = n_ctx // (CHUNK * cps)

    def fwd(i):
        return i

    def bwd(i):
        return jnp.where(i < ctx_blocks, ctx_blocks - 1 - i, n_blocks + ctx_blocks - 1 - i)

    return n_blocks, fwd, bwd


def _mod_kernel(cc_ref, w_ref, b_ref, o_ref):
    s = _silu(cc_ref[...])
    o_ref[...] = _dot3(s, w_ref[...]) + b_ref[...]


def _modulation(cc, w_mod, b_mod):
    depth = w_mod.shape[0]
    ncol = w_mod.shape[2] // D_MODEL
    return pl.pallas_call(
        _mod_kernel,
        grid=(depth, ncol),
        in_specs=[
            pl.BlockSpec((SUBLANES, D_MODEL), lambda l, j: (0, 0)),
            pl.BlockSpec((None, D_MODEL, D_MODEL), lambda l, j: (l, 0, j)),
            pl.BlockSpec((None, 1, D_MODEL), lambda l, j: (l, 0, j)),
        ],
        out_specs=pl.BlockSpec((None, SUBLANES, D_MODEL), lambda l, j: (l, 0, j)),
        out_shape=jax.ShapeDtypeStruct((depth, SUBLANES, w_mod.shape[2]), F32),
        compiler_params=_params(("parallel", "parallel")),
        name="modulation",
    )(cc, w_mod, b_mod.reshape(depth, 1, -1))


def _seg_vec(mod_ref, idx, row0, nrows, n_ctx):
    lo = idx * D_MODEL
    ctx = mod_ref[0:1, lo:lo + D_MODEL]
    lat = mod_ref[1:2, lo:lo + D_MODEL]
    is_ctx = (_iota((nrows, 1), 0) + row0) < n_ctx
    return jnp.where(is_ctx, ctx, lat)


def _rmsnorm_rows(x, g):
    ms = jnp.mean(x * x, axis=-1, keepdims=True)
    return x * lax.rsqrt(ms + NORM_EPS) * g


def _norm_kernel(x_ref, g_ref, mod_ref, o_ref, *, n_ctx, tb, shift_idx, scale_idx):
    row0 = pl.program_id(0) * tb
    xn = _rmsnorm_rows(x_ref[...], g_ref[...])
    if scale_idx is not None:
        xn = xn * (1.0 + _seg_vec(mod_ref, scale_idx, row0, tb, n_ctx)) \
            + _seg_vec(mod_ref, shift_idx, row0, tb, n_ctx)
    o_ref[...] = xn.astype(o_ref.dtype)


def _norm(x, g, mod, n_ctx, tb, out_dtype, shift_idx=0, scale_idx=1):
    n = x.shape[0]
    return pl.pallas_call(
        functools.partial(_norm_kernel, n_ctx=n_ctx, tb=tb, shift_idx=shift_idx, scale_idx=scale_idx),
        grid=(n // tb,),
        in_specs=[
            pl.BlockSpec((tb, D_MODEL), lambda i: (i, 0)),
            pl.BlockSpec((1, D_MODEL), lambda i: (0, 0)),
            pl.BlockSpec(mod.shape, lambda i: (0, 0)),
        ],
        out_specs=pl.BlockSpec((tb, D_MODEL), lambda i: (i, 0)),
        out_shape=jax.ShapeDtypeStruct((n, D_MODEL), out_dtype),
        compiler_params=_params(("parallel",)),
        name="adaln_norm",
    )(x, g, mod)


def _mm_kernel(a_ref, w_ref, o_ref):
    o_ref[...] = _dg(a_ref[...], w_ref[...])


def _project(a, w, tm, name):
    n, k = a.shape
    cols = w.shape[1]
    return pl.pallas_call(
        _mm_kernel,
        grid=(n // tm,),
        in_specs=[pl.BlockSpec((tm, k), lambda i: (i, 0)),
                  pl.BlockSpec((k, cols), lambda i: (0, 0))],
        out_specs=pl.BlockSpec((tm, cols), lambda i: (i, 0)),
        out_shape=jax.ShapeDtypeStruct((n, cols), F32),
        compiler_params=_params(("parallel",)),
        name=name,
    )(a, w)


def _fill_ext(ext_ref, cur_ref, prev_ref, next_ref, has_prev, has_next, tb):
    zeros = jnp.zeros(prev_ref.shape, F32)
    ext_ref[0:SUBLANES, :] = jnp.where(has_prev, prev_ref[...], zeros)
    ext_ref[SUBLANES:SUBLANES + tb, :] = cur_ref[...]
    ext_ref[SUBLANES + tb:SUBLANES + tb + SUBLANES, :] = jnp.where(has_next, next_ref[...], zeros)


def _halo_maps(tb, nblk):
    r = tb // SUBLANES

    def prev_map(i):
        return (jnp.maximum(i * r - 1, 0), 0)

    def next_map(i):
        return (jnp.minimum((i + 1) * r, nblk * r - 1), 0)

    return prev_map, next_map


def _rw_prep_kernel(cur_ref, prev_ref, next_ref, mu_ref, w0_ref, w2_ref, a0_ref, a2_ref, g2_ref,
                    kka_ref, bd_ref,
                    r_ref, v_ref, kk_ref, lw_ref, kd_ref, bdir_ref, g_ref, bonus_ref,
                    *, tb, ctx_blocks, nblk):
    i = pl.program_id(0)
    seg_first = jnp.logical_or(i == 0, i == ctx_blocks)
    seg_last = jnp.logical_or(i == ctx_blocks - 1, i == nblk - 1)
    x = cur_ref[...]
    row = _iota((tb, 1), 0)
    prev_row = jnp.where(seg_first, 0.0, prev_ref[SUBLANES - 1:SUBLANES, :])
    next_row = jnp.where(seg_last, 0.0, next_ref[0:1, :])
    x_prev = jnp.where(row == 0, prev_row, pltpu.roll(x, 1, axis=0))
    x_next = jnp.where(row == tb - 1, next_row, pltpu.roll(x, tb - 1, axis=0))
    xs = mu_ref[0:1, :] * x_prev + mu_ref[1:2, :] * x + mu_ref[2:3, :] * x_next
    bw = BRANCH_W
    r = xs[:, 0:bw]
    k = xs[:, bw:2 * bw]
    v = xs[:, 2 * bw:3 * bw]
    wl = xs[:, 3 * bw:3 * bw + LANES]
    al = xs[:, 3 * bw + LANES:3 * bw + 2 * LANES]
    gl = xs[:, 3 * bw + 2 * LANES:3 * bw + 3 * LANES]
    bd = bd_ref[...]
    k_k = kka_ref[0:1, :]
    k_a = kka_ref[1:2, :]
    r_k = kka_ref[2:3, :]

    kk0 = k * k_k
    kk = kk0 * lax.rsqrt(_dotxc(kk0 * kk0, bd) + 1e-12)
    g = _dot3(_sigmoid(gl), g2_ref[...])
    twl = jnp.tanh(wl)
    ksum = jnp.zeros_like(k)
    for z in range(2):
        wlog = w0_ref[z:z + 1, :] + _dot3(twl, w2_ref[z])
        w = -_softplus(-wlog) - 0.5
        lw = -jnp.exp(w)
        a = _sigmoid(a0_ref[z:z + 1, :] + _dot3(al, a2_ref[z]))
        kd = k * (1.0 + (a - 1.0) * k_a)
        bdir = kk * a
        ksum = ksum + kd
        for p in range(RW_PAIRS):
            sl = slice(p * LANES, (p + 1) * LANES)
            lw_ref[z, p] = lw[:, sl]
            kd_ref[z, p] = kd[:, sl]
            bdir_ref[z, p] = bdir[:, sl]
    bonus = _dotxc(r * ksum * r_k, bd) * v
    for p in range(RW_PAIRS):
        sl = slice(p * LANES, (p + 1) * LANES)
        r_ref[p] = r[:, sl]
        v_ref[p] = v[:, sl]
        kk_ref[p] = kk[:, sl]
        g_ref[p] = g[:, sl]
        bonus_ref[p] = bonus[:, sl]


def _rw_prep(p_rw, wts, n_ctx, tb):
    n = p_rw.shape[0]
    nblk = n // tb
    prev_map, next_map = _halo_maps(tb, nblk)
    full = lambda a: pl.BlockSpec(a.shape, lambda i: (0,) * a.ndim)
    pair_spec = pl.BlockSpec((RW_PAIRS, tb, LANES), lambda i: (0, i, 0))
    dir_spec = pl.BlockSpec((2, RW_PAIRS, tb, LANES), lambda i: (0, 0, i, 0))
    pair_shape = jax.ShapeDtypeStruct((RW_PAIRS, n, LANES), F32)
    dir_shape = jax.ShapeDtypeStruct((2, RW_PAIRS, n, LANES), F32)
    params = [wts["mu3"], wts["w0"], wts["w2p"], wts["a0"], wts["a2p"], wts["g2"], wts["kka"], wts["bd64"]]
    return pl.pallas_call(
        functools.partial(_rw_prep_kernel, tb=tb, ctx_blocks=n_ctx // tb, nblk=nblk),
        grid=(nblk,),
        in_specs=[pl.BlockSpec((tb, RW_COLS), lambda i: (i, 0)),
                  pl.BlockSpec((SUBLANES, RW_COLS), prev_map),
                  pl.BlockSpec((SUBLANES, RW_COLS), next_map)]
                 + [full(a) for a in params],
        out_specs=[pair_spec, pair_spec, pair_spec, dir_spec, dir_spec, dir_spec, pair_spec, pair_spec],
        out_shape=[pair_shape, pair_shape, pair_shape, dir_shape, dir_shape, dir_shape, pair_shape, pair_shape],
        compiler_params=_params(("parallel",)),
        name="rwkv_prep",
    )(p_rw, p_rw, p_rw, *params)


def _rw_chunk_group(items, m_ref):
    n = CHUNK
    prow = _iota((n, LANES), 0)
    pcol = _iota((n, LANES), 1) % n
    masks = {False: (pcol <= prow, pcol < prow), True: (pcol >= prow, pcol > prow)}
    head_a = _iota((n, LANES), 1) < RW_HEAD
    head_a2 = jnp.concatenate([head_a, head_a], axis=1)
    row = _iota((LANES, LANES), 0)
    col = _iota((LANES, LANES), 1)
    same_head = (row < RW_HEAD) == (col < RW_HEAD)
    diag = row == col

    for d in items:
        r, v, kk, lw, kd, bdir = d["tile"]
        incl, _ = _order_masks(n, d["rev"])
        cum = _dotc(incl.astype(BF16), lw)
        cum_end = cum[0:1, :] if d["rev"] else cum[n - 1:n, :]
        e_inv = jnp.exp(-cum)
        e_end = jnp.exp(cum_end - cum)
        d.update(v=v, kkd=kk * jnp.exp(cum - lw), kinv=kd * e_inv, binv=bdir * e_inv, rd=r * jnp.exp(cum),
                 kend=kd * e_end, bend=bdir * e_end, g_end=jnp.exp(cum_end))

    for d in items:
        incl2, strict2 = masks[d["rev"]]
        lhs = jnp.concatenate([d["kkd"], d["rd"]], axis=0)
        with_b = _dot1(lhs, _pair_rows(d["binv"], head_a), NT)
        with_k = _dot1(lhs, _pair_rows(d["kinv"], head_a), NT)
        d["a_kb"] = jnp.where(strict2, with_b[:n], 0.0)
        d["a_rb"] = jnp.where(incl2, with_b[n:], 0.0)
        d["a_kk"] = jnp.where(strict2, with_k[:n], 0.0)
        d["a_rk"] = jnp.where(incl2, with_k[n:], 0.0)
        d["v_rows"] = _pair_rows(d["v"], head_a).astype(BF16)
    t = _tri_inverse_pairs([d["a_kb"] for d in items])
    for d, ti in zip(items, t):
        akv = _dot1(d["a_kk"], d["v_rows"])
        d["uw"] = _dot1(ti, _pair_rows(jnp.concatenate([akv, d["kkd"]], axis=1), head_a2))
    for d in items:
        rb_uw = _dot1(d["a_rb"], _pair_rows(d["uw"], head_a2))
        d["y0"] = _dot1(d["a_rk"], d["v_rows"]) - rb_uw[:, :LANES]
        d["qp"] = d["rd"] - rb_uw[:, LANES:]
        b_uw = _dot1(d["bend"], d["uw"], TN)
        p_mat = jnp.where(diag, jnp.broadcast_to(d["g_end"], (LANES, LANES)), 0.0) - b_uw[:, LANES:]
        d["p_mat"] = jnp.where(same_head, p_mat, 0.0)
        d["z0"] = _dot1(d["kend"], d["v"], TN) - b_uw[:, :LANES]

    for z, p in sorted({(d["z"], d["p"]) for d in items}):
        mine = sorted((d for d in items if (d["z"], d["p"]) == (z, p)), key=lambda d: d["c"], reverse=bool(z))
        m = m_ref[z, p]
        for d in mine:
            qm_pm = _dot3(jnp.concatenate([d["qp"], d["p_mat"]], axis=0), m)
            d["y_ref"][p, d["rows"], :] = d["y0"] + qm_pm[:n]
            m = jnp.where(same_head, qm_pm[n:] + d["z0"], 0.0)
        m_ref[z, p] = m


def _rw_chunk_kernel(rf, vf, kkf, lwf, kdf, bf, rb, vb, kkb, lwb, kdb, bb, yf_ref, yb_ref, m_ref):
    @pl.when(pl.program_id(0) == 0)
    def _():
        m_ref[...] = jnp.zeros(m_ref.shape, F32)

    dirs = ((rf, vf, kkf, lwf, kdf, bf, yf_ref, False), (rb, vb, kkb, lwb, kdb, bb, yb_ref, True))
    items = []
    for z, (r, v, kk, lw, kd, b, y_ref, rev) in enumerate(dirs):
        for p in range(RW_PAIRS):
            for c in range(RW_CPS):
                rows = slice(c * CHUNK, (c + 1) * CHUNK)
                items.append(dict(z=z, p=p, c=c, rev=rev, rows=rows, y_ref=y_ref,
                                  tile=tuple(x[p, rows, :] for x in (r, v, kk, lw, kd, b))))
    _rw_chunk_group(items, m_ref)


def _rw_chunk(r, v, kk, lw, kd, bdir, n_ctx):
    n = r.shape[1]
    n_blocks, fwd, bwd = _chunk_maps(n, n_ctx, RW_CPS)
    rows = CHUNK * RW_CPS

    def shared(cm):
        return pl.BlockSpec((RW_PAIRS, rows, LANES), lambda i: (0, cm(i), 0))

    def per_dir(z, cm):
        return pl.BlockSpec((None, RW_PAIRS, rows, LANES), lambda i: (z, 0, cm(i), 0))

    in_specs = [shared(fwd), shared(fwd), shared(fwd), per_dir(0, fwd), per_dir(0, fwd), per_dir(0, fwd),
                shared(bwd), shared(bwd), shared(bwd), per_dir(1, bwd), per_dir(1, bwd), per_dir(1, bwd)]
    shape = jax.ShapeDtypeStruct((RW_PAIRS, n, LANES), F32)
    return pl.pallas_call(
        _rw_chunk_kernel,
        grid=(n_blocks,),
        in_specs=in_specs,
        out_specs=[shared(fwd), shared(bwd)],
        out_shape=[shape, shape],
        scratch_shapes=[pltpu.VMEM((2, RW_PAIRS, LANES, LANES), F32)],
        compiler_params=_params(("arbitrary",)),
        name="rwkv_chunk",
    )(r, v, kk, lw, kd, bdir, r, v, kk, lw, kd, bdir)


def _gla_chunk_kernel(qf, kf, vf, alf, qb, kb, vb, alb, a2_ref, ab_ref, of_ref, ob_ref, s_ref):
    @pl.when(pl.program_id(0) == 0)
    def _():
        s_ref[...] = jnp.zeros(s_ref.shape, F32)

    n = CHUNK
    head_a = _iota((n, LANES), 1) < GLA_DK
    top = _iota((LANES, LANES), 0) < GLA_DK
    items = []
    for z, (q_ref, k_ref, v_ref, al_ref, o_ref) in enumerate(((qf, kf, vf, alf, of_ref), (qb, kb, vb, alb, ob_ref))):
        rev = bool(z)
        incl, _ = _order_masks(n, rev)
        for c in range(GLA_CPS):
            rows = slice(c * n, (c + 1) * n)
            zlog = _dot3(al_ref[rows, :], a2_ref[z]) + ab_ref[z:z + 1, :]
            log_a = (jnp.minimum(zlog, 0.0) - jnp.log(1.0 + jnp.exp(-jnp.abs(zlog)))) * (1.0 / GLA_GATE_NORM)
            bcum = _dotc(incl.astype(BF16), log_a)
            b_end = bcum[0:1, :] if rev else bcum[n - 1:n, :]
            q_dec = q_ref[rows, :] * (GLA_DK ** -0.5) * jnp.exp(bcum)
            k_inv = k_ref[rows, :] * jnp.exp(-bcum)
            k_end = k_ref[rows, :] * jnp.exp(b_end - bcum)
            dec_end = jnp.exp(b_end)
            for p in range(GLA_HEADS // 2):
                sl = slice(p * LANES, (p + 1) * LANES)
                items.append(dict(z=z, c=c, p=p, incl=incl, rows=rows, o_ref=o_ref, qd=q_dec[:, sl], ki=k_inv[:, sl],
                                  ke=k_end[:, sl], dec=dec_end[:, sl],
                                  v=[v_ref[rows, (2 * p + h) * GLA_DV:(2 * p + h + 1) * GLA_DV] for h in range(2)]))
    for d in items:
        d["qd_h"] = [jnp.where(head_a if h == 0 else jnp.logical_not(head_a), d["qd"], 0.0) for h in range(2)]
        d["att"] = [jnp.where(d["incl"], _dot1(d["qd_h"][h], d["ki"], NT), 0.0) for h in range(2)]
    for d in items:
        d["intra"] = [_dot1(d["att"][h], d["v"][h]) for h in range(2)]
        upd = [_dot1(d["ke"], d["v"][h], TN) for h in range(2)]
        d["upd"] = jnp.where(top, upd[0], upd[1])
        d["dec_col"] = jnp.transpose(jnp.broadcast_to(d["dec"], (LANES, LANES)))
    for z in range(2):
        order = range(GLA_CPS - 1, -1, -1) if z else range(GLA_CPS)
        for p in range(GLA_HEADS // 2):
            s = s_ref[z, p]
            for c in order:
                d = next(x for x in items if (x["z"], x["c"], x["p"]) == (z, c, p))
                inter = _dot3(jnp.concatenate(d["qd_h"], axis=0), s)
                for h in range(2):
                    hh = 2 * p + h
                    d["o_ref"][d["rows"], hh * GLA_DV:(hh + 1) * GLA_DV] = d["intra"][h] + inter[h * n:(h + 1) * n]
                s = d["dec_col"] * s + d["upd"]
            s_ref[z, p] = s


def _gla_chunk(p_gla, a2p, ab, n_ctx):
    n = p_gla.shape[0]
    n_blocks, fwd, bwd = _chunk_maps(n, n_ctx, GLA_CPS)
    hk = GLA_HEADS * GLA_DK
    rows = CHUNK * GLA_CPS

    def specs(cm):
        return [pl.BlockSpec((rows, hk), lambda i: (cm(i), 0)),
                pl.BlockSpec((rows, hk), lambda i: (cm(i), 1)),
                pl.BlockSpec((rows, BRANCH_W), lambda i: (cm(i), 1)),
                pl.BlockSpec((rows, LANES), lambda i: (cm(i), (2 * hk + 2 * BRANCH_W) // LANES))]

    full = lambda a: pl.BlockSpec(a.shape, lambda i: (0,) * a.ndim)
    shape = jax.ShapeDtypeStruct((n, BRANCH_W), F32)
    return pl.pallas_call(
        _gla_chunk_kernel,
        grid=(n_blocks,),
        in_specs=specs(fwd) + specs(bwd) + [full(a2p), full(ab)],
        out_specs=[pl.BlockSpec((rows, BRANCH_W), lambda i: (fwd(i), 0)),
                   pl.BlockSpec((rows, BRANCH_W), lambda i: (bwd(i), 0))],
        out_shape=[shape, shape],
        scratch_shapes=[pltpu.VMEM((2, GLA_HEADS // 2, LANES, LANES), F32)],
        compiler_params=_params(("arbitrary",)),
        name="gla_chunk",
    )(p_gla, p_gla, p_gla, p_gla, p_gla, p_gla, p_gla, p_gla, a2p, ab)


def _gated_norm_kernel(of_ref, ob_ref, gate_ref, g_ref, o_ref):
    for h in range(BRANCH_W // LANES):
        sl = slice(h * LANES, (h + 1) * LANES)
        o = of_ref[:, sl] + ob_ref[:, sl]
        ms = jnp.mean(o * o, axis=-1, keepdims=True)
        o_ref[:, sl] = o * lax.rsqrt(ms + NORM_EPS) * g_ref[...] * _silu(gate_ref[:, sl])


def _gated_norm(of, ob, gate_src, gate_block, norm_g, tb, name):
    n = of.shape[0]
    spec = pl.BlockSpec((tb, BRANCH_W), lambda i: (i, 0))
    return pl.pallas_call(
        _gated_norm_kernel,
        grid=(n // tb,),
        in_specs=[spec, spec, pl.BlockSpec((tb, BRANCH_W), lambda i: (i, gate_block)),
                  pl.BlockSpec((1, LANES), lambda i: (0, 0))],
        out_specs=spec,
        out_shape=jax.ShapeDtypeStruct((n, BRANCH_W), F32),
        compiler_params=_params(("parallel",)),
        name=name,
    )(of, ob, gate_src, norm_g)


def _gdn_prep_kernel(cur_ref, prev_ref, next_ref, sc_ref, cw_ref, gp_ref,
                     q_ref, k_ref, v_ref, gb_ref, ext_ref, *, tb, ctx_blocks, nblk):
    i = pl.program_id(0)
    seg_first = jnp.logical_or(i == 0, i == ctx_blocks)
    seg_last = jnp.logical_or(i == ctx_blocks - 1, i == nblk - 1)
    _fill_ext(ext_ref, cur_ref, prev_ref, next_ref, jnp.logical_not(seg_first),
              jnp.logical_not(seg_last), tb)
    pad = GDN_CONV // 2
    acc = cw_ref[0:1, :] * ext_ref[pl.ds(SUBLANES - pad, tb), :]
    for t in range(1, GDN_CONV):
        acc = acc + cw_ref[t:t + 1, :] * ext_ref[pl.ds(SUBLANES - pad + t, tb), :]
    qkv = _silu(acc)
    for h in range(GDN_HEADS):
        for j, ref in enumerate((q_ref, k_ref, v_ref)):
            sl = slice(j * BRANCH_W + h * GDN_HEAD, j * BRANCH_W + (h + 1) * GDN_HEAD)
            osl = slice(h * GDN_HEAD, (h + 1) * GDN_HEAD)
            x = qkv[:, sl]
            if j < 2:
                x = x * lax.rsqrt(jnp.sum(x * x, axis=-1, keepdims=True) + 1e-12)
            if j == 0:
                x = x * (GDN_HEAD ** -0.5)
            ref[:, osl] = x
    sc = sc_ref[...]
    lane = _iota(sc.shape, 1)
    nh2 = 2 * GDN_HEADS
    g = -jnp.exp(gp_ref[0:1, :]) * _softplus(sc + gp_ref[1:2, :])
    beta = _sigmoid(sc)
    gb_ref[...] = jnp.where(lane < nh2, g, jnp.where(lane < 2 * nh2, beta, 0.0))


def _gdn_prep(p_gdn, conv_w, gparams, n_ctx, tb):
    n = p_gdn.shape[0]
    nblk = n // tb
    qkv_cols = 3 * BRANCH_W
    prev_map, next_map = _halo_maps(tb, nblk)
    sc_block = (4 * BRANCH_W) // LANES
    full = lambda a: pl.BlockSpec(a.shape, lambda i: (0,) * a.ndim)
    shape = jax.ShapeDtypeStruct((n, BRANCH_W), F32)
    spec = pl.BlockSpec((tb, BRANCH_W), lambda i: (i, 0))
    return pl.pallas_call(
        functools.partial(_gdn_prep_kernel, tb=tb, ctx_blocks=n_ctx // tb, nblk=nblk),
        grid=(nblk,),
        in_specs=[pl.BlockSpec((tb, qkv_cols), lambda i: (i, 0)),
                  pl.BlockSpec((SUBLANES, qkv_cols), prev_map),
                  pl.BlockSpec((SUBLANES, qkv_cols), next_map),
                  pl.BlockSpec((tb, LANES), lambda i: (i, sc_block)),
                  full(conv_w), full(gparams)],
        out_specs=[spec, spec, spec, pl.BlockSpec((tb, LANES), lambda i: (i, 0))],
        out_shape=[shape, shape, shape, jax.ShapeDtypeStruct((n, LANES), F32)],
        scratch_shapes=[pltpu.VMEM((tb + 2 * SUBLANES, qkv_cols), F32)],
        compiler_params=_params(("parallel",)),
        name="gdn_prep",
    )(p_gdn, p_gdn, p_gdn, p_gdn, conv_w, gparams)


def _gdn_chunk_kernel(qf, kf, vf, gbf, qb, kb, vb, gbb, of_ref, ob_ref, s_ref):
    @pl.when(pl.program_id(0) == 0)
    def _():
        s_ref[...] = jnp.zeros(s_ref.shape, F32)

    n = CHUNK
    w2 = 2 * GDN_HEAD
    lane = _iota((n, LANES), 1)
    head_a = lane < n
    first2 = _iota((n, w2), 1) < GDN_HEAD
    prow = _iota((n, LANES), 0)
    pcol = lane % n
    diag = _iota((LANES, LANES), 0) == _iota((LANES, LANES), 1)
    ones = jnp.ones((n, LANES), BF16)
    blocks = []
    for z, (q_ref, k_ref, v_ref, gb_ref, o_ref) in enumerate(((qf, kf, vf, gbf, of_ref), (qb, kb, vb, gbb, ob_ref))):
        rev = bool(z)
        incl_sq, _ = _order_masks(n, rev)
        for c in range(GDN_CPS):
            rows = slice(c * n, (c + 1) * n)
            blocks.append(dict(z=z, c=c, rev=rev, rows=rows, refs=(q_ref, k_ref, v_ref, o_ref), gb=gb_ref[rows, :],
                               tri=incl_sq.astype(BF16)))
    for b in blocks:
        b["gam_all"] = _dotc(b["tri"], b["gb"])
    for b in blocks:
        b["gam_parts"] = _split3(b["gam_all"])
    items = []
    for b in blocks:
        z, rev, rows, gb, gam_all = b["z"], b["rev"], b["rows"], b["gb"], b["gam_all"]
        q_ref, k_ref, v_ref, o_ref = b["refs"]
        incl = (pcol >= prow) if rev else (pcol <= prow)
        strict = (pcol > prow) if rev else (pcol < prow)
        for hp in range(GDN_HEADS // 2):
            sl = slice(hp * w2, (hp + 1) * w2)
            js = [z * GDN_HEADS + 2 * hp + i for i in range(2)]
            gam_h = [jnp.broadcast_to(gam_all[:, j:j + 1], (n, LANES)) for j in js]
            beta_h = [jnp.broadcast_to(gb[:, 2 * GDN_HEADS + j:2 * GDN_HEADS + j + 1], (n, LANES)) for j in js]
            pick = [jnp.concatenate([jnp.where(lane == js[0], part, 0.0).astype(BF16),
                                     jnp.where(lane == js[1], part, 0.0).astype(BF16)], axis=0)
                    for part in b["gam_parts"]]
            gam2 = jnp.concatenate(gam_h, axis=1)
            items.append(dict(z=z, c=b["c"], hp=hp, rows=rows, o_ref=o_ref, strict=strict, incl=incl,
                              pick=pick, gam_col=jnp.where(head_a, gam_h[0], gam_h[1]),
                              q=q_ref[rows, sl], k=k_ref[rows, sl], v=v_ref[rows, sl], gam2=gam2,
                              gam_end2=gam2[0:1, :] if rev else gam2[n - 1:n, :],
                              beta2=jnp.concatenate(beta_h, axis=1),
                              beta_p=jnp.where(head_a, beta_h[0], beta_h[1])))
    for d in items:
        d["gam_row"] = sum(_dg(ones, pk, NT) for pk in d["pick"])
    for d in items:
        d["decay"] = jnp.where(d["incl"], jnp.exp(jnp.where(d["incl"], d["gam_col"] - d["gam_row"], 0.0)), 0.0)

    for d in items:
        d["e_gam2"] = jnp.exp(d["gam2"])
        kq = _dot1(jnp.concatenate([d["k"], d["q"]], axis=0), _pair_rows(d["k"], first2), NT)
        d["a"] = jnp.where(d["strict"], d["beta_p"] * kq[:n] * d["decay"], 0.0)
        d["qk"] = kq[n:] * d["decay"]
    t = _tri_inverse_pairs([d["a"] for d in items])

    def heads_of(m):
        return m[:, :n], pltpu.roll(m, n, axis=1)[:, :n]

    for d, ti in zip(items, t):
        bv = d["beta2"] * d["v"]
        bek = d["beta2"] * d["e_gam2"] * d["k"]
        d["uw"] = []
        for i, t_h in enumerate(heads_of(ti)):
            hs = slice(i * GDN_HEAD, (i + 1) * GDN_HEAD)
            d["uw"].append(_dot1(t_h, jnp.concatenate([bv[:, hs], bek[:, hs]], axis=1)))
    for d in items:
        k_end = d["k"] * jnp.exp(d["gam_end2"] - d["gam2"])
        dec_end = jnp.exp(d["gam_end2"])
        q_dec = d["q"] * d["e_gam2"]
        d["p"], d["z0"], d["y0"], d["qp"] = [], [], [], []
        for i, qk_h in enumerate(heads_of(d["qk"])):
            hs = slice(i * GDN_HEAD, (i + 1) * GDN_HEAD)
            qx = _dot1(qk_h, d["uw"][i])
            d["y0"].append(qx[:, :LANES])
            d["qp"].append(q_dec[:, hs] - qx[:, LANES:])
            ke_uw = _dot1(k_end[:, hs], d["uw"][i], TN)
            d["p"].append(jnp.where(diag, jnp.broadcast_to(dec_end[:, hs], (LANES, LANES)), 0.0) - ke_uw[:, LANES:])
            d["z0"].append(ke_uw[:, :LANES])
    for z in range(2):
        order = range(GDN_CPS - 1, -1, -1) if z else range(GDN_CPS)
        for hp in range(GDN_HEADS // 2):
            for i in range(2):
                h = 2 * hp + i
                s = s_ref[z, h]
                for c in order:
                    d = next(x for x in items if (x["z"], x["hp"], x["c"]) == (z, hp, c))
                    qs_ps = _dot3(jnp.concatenate([d["qp"][i], d["p"][i]], axis=0), s)
                    d["o_ref"][d["rows"], h * GDN_HEAD:(h + 1) * GDN_HEAD] = d["y0"][i] + qs_ps[:n]
                    s = qs_ps[n:] + d["z0"][i]
                s_ref[z, h] = s


def _gdn_chunk(q, k, v, gb, n_ctx):
    n = q.shape[0]
    n_blocks, fwd, bwd = _chunk_maps(n, n_ctx, GDN_CPS)
    rows = CHUNK * GDN_CPS

    def specs(cm):
        wide = pl.BlockSpec((rows, BRANCH_W), lambda i: (cm(i), 0))
        return [wide, wide, wide, pl.BlockSpec((rows, LANES), lambda i: (cm(i), 0))]

    shape = jax.ShapeDtypeStruct((n, BRANCH_W), F32)
    return pl.pallas_call(
        _gdn_chunk_kernel,
        grid=(n_blocks,),
        in_specs=specs(fwd) + specs(bwd),
        out_specs=[pl.BlockSpec((rows, BRANCH_W), lambda i: (fwd(i), 0)),
                   pl.BlockSpec((rows, BRANCH_W), lambda i: (bwd(i), 0))],
        out_shape=[shape, shape],
        scratch_shapes=[pltpu.VMEM((2, GDN_HEADS, GDN_HEAD, GDN_HEAD), F32)],
        compiler_params=_params(("arbitrary",)),
        name="gdn_chunk",
    )(q, k, v, gb, q, k, v, gb)


def _merge_kernel(hn_ref, yf_ref, yb_ref, bonus_ref, g_ref, ln_ref, bd_ref, ygla_ref, of_ref, ob_ref, zg_ref,
                  gng_ref, h_ref, wg_ref, wb_ref, wo_ref, mod_ref, o_ref, *, n_ctx, tb):
    hn = hn_ref[...]
    bd = bd_ref[...]
    inv_n = 1.0 / RW_HEAD
    rw = jnp.zeros((tb, D_MODEL), F32)
    for p in range(RW_PAIRS):
        sl = slice(p * LANES, (p + 1) * LANES)
        y = yf_ref[p] + yb_ref[p]
        mu = _dotxc(y, bd) * inv_n
        d = y - mu
        var = _dotxc(d * d, bd) * inv_n
        yn = d * lax.rsqrt(var + RW_GN_EPS) * ln_ref[0:1, sl] + ln_ref[1:2, sl]
        y_rw = (yn + bonus_ref[p]) * g_ref[p]
        rw = rw + _dg(y_rw.astype(BF16), wb_ref[0, sl, :])
    gdn = jnp.zeros((tb, D_MODEL), F32)
    for hd in range(GDN_HEADS):
        sl = slice(hd * GDN_HEAD, (hd + 1) * GDN_HEAD)
        o = of_ref[:, sl] + ob_ref[:, sl]
        ms = jnp.mean(o * o, axis=-1, keepdims=True)
        y_gdn = o * lax.rsqrt(ms + NORM_EPS) * gng_ref[...] * _silu(zg_ref[:, sl])
        gdn = gdn + _dg(y_gdn.astype(BF16), wb_ref[2, sl, :])
    gla = _dg(ygla_ref[...].astype(BF16), wb_ref[1])
    merged = jnp.zeros((tb, D_MODEL), F32)
    for g, branch in enumerate((rw, gla, gdn)):
        gate = _sigmoid(_dg(hn, wg_ref[:, g * D_MODEL:(g + 1) * D_MODEL]))
        merged = merged + branch * gate
    mix = _dg(merged.astype(BF16), wo_ref[...])
    gate_vec = _seg_vec(mod_ref, 2, pl.program_id(0) * tb, tb, n_ctx)
    o_ref[...] = h_ref[...] + mix * gate_vec


def _merge(hn, yf, yb, bonus, g, ln, bd128, y_gla, of, ob, p_gdn, gdn_norm_g, h, wg, wb, wo, mod, n_ctx, tb):
    n = h.shape[0]
    full = lambda a: pl.BlockSpec(a.shape, lambda i: (0,) * a.ndim)
    wide = pl.BlockSpec((tb, D_MODEL), lambda i: (i, 0))
    half = pl.BlockSpec((tb, BRANCH_W), lambda i: (i, 0))
    pair = pl.BlockSpec((RW_PAIRS, tb, LANES), lambda i: (0, i, 0))
    zg = pl.BlockSpec((tb, BRANCH_W), lambda i: (i, 3))
    return pl.pallas_call(
        functools.partial(_merge_kernel, n_ctx=n_ctx, tb=tb),
        grid=(n // tb,),
        in_specs=[wide, pair, pair, pair, pair, full(ln), full(bd128), half, half, half, zg, full(gdn_norm_g),
                  wide, full(wg), full(wb), full(wo), full(mod)],
        out_specs=wide,
        out_shape=jax.ShapeDtypeStruct((n, D_MODEL), F32),
        compiler_params=_params(("parallel",)),
        name="merge_out",
    )(hn, yf, yb, bonus, g, ln, bd128, y_gla, of, ob, p_gdn, gdn_norm_g, h, wg, wb, wo, mod)


def _mlp_kernel(h_ref, g_ref, mod_ref, w1_ref, w2_ref, gn_ref, modn_ref, *rest, n_ctx, tb, nj, last):
    if last:
        o_ref, hn_ref, acc_ref = rest
    else:
        o_ref, hnext_ref, hn_ref, acc_ref = rest
    j = pl.program_id(1)
    row0 = pl.program_id(0) * tb

    @pl.when(j == 0)
    def _():
        xn = _rmsnorm_rows(h_ref[...], g_ref[...])
        xn = xn * (1.0 + _seg_vec(mod_ref, 4, row0, tb, n_ctx)) + _seg_vec(mod_ref, 3, row0, tb, n_ctx)
        hn_ref[...] = xn.astype(BF16)
        acc_ref[...] = jnp.zeros(acc_ref.shape, F32)

    hid = jnp.maximum(_dg(hn_ref[...], w1_ref[...]), 0.0)
    acc_ref[...] += _dg((hid * hid).astype(BF16), w2_ref[...])

    @pl.when(j == nj - 1)
    def _():
        h_new = h_ref[...] + acc_ref[...] * _seg_vec(mod_ref, 5, row0, tb, n_ctx)
        nxt = _rmsnorm_rows(h_new, gn_ref[...])
        if last:
            o_ref[...] = nxt
        else:
            o_ref[...] = h_new
            nxt = nxt * (1.0 + _seg_vec(modn_ref, 1, row0, tb, n_ctx)) + _seg_vec(modn_ref, 0, row0, tb, n_ctx)
            hnext_ref[...] = nxt.astype(BF16)


def _mlp(h, g, mod, w1, w2, g_next, mod_next, n_ctx, tb, hb, last):
    n = h.shape[0]
    nj = MLP_HIDDEN // hb
    wide = pl.BlockSpec((tb, D_MODEL), lambda i, j: (i, 0))
    vec = pl.BlockSpec((1, D_MODEL), lambda i, j: (0, 0))
    out_specs = [wide] if last else [wide, wide]
    out_shape = [jax.ShapeDtypeStruct((n, D_MODEL), F32)]
    if not last:
        out_shape.append(jax.ShapeDtypeStruct((n, D_MODEL), BF16))
    return pl.pallas_call(
        functools.partial(_mlp_kernel, n_ctx=n_ctx, tb=tb, nj=nj, last=last),
        grid=(n // tb, nj),
        in_specs=[wide, vec, pl.BlockSpec(mod.shape, lambda i, j: (0, 0)),
                  pl.BlockSpec((D_MODEL, hb), lambda i, j: (0, j)),
                  pl.BlockSpec((hb, D_MODEL), lambda i, j: (j, 0)),
                  vec, pl.BlockSpec(mod_next.shape, lambda i, j: (0, 0))],
        out_specs=out_specs,
        out_shape=out_shape,
        scratch_shapes=[pltpu.VMEM((tb, D_MODEL), BF16), pltpu.VMEM((tb, D_MODEL), F32)],
        compiler_params=_params(("parallel", "arbitrary")),
        name="mlp",
    )(h, g, mod, w1, w2, g_next, mod_next)


def _block_diag_ones(n, blk):
    r = jnp.arange(n) // blk
    return (r[:, None] == r[None, :]).astype(BF16)


def _pad_cols(w, cols):
    return jnp.pad(w, ((0, 0), (0, cols - w.shape[1])))


def _lora_pad(w2, lora):
    out = jnp.zeros((2, LANES, w2.shape[2]), F32)
    for z in range(2):
        out = out.at[z, z * lora:(z + 1) * lora].set(w2[z])
    return out


def _pick_block(n, n_ctx, candidates):
    for c in candidates:
        if n % c == 0 and n_ctx % c == 0:
            return c
    raise ValueError("token counts must be multiples of the chunk length")


def _pick_rows(n, candidates):
    for c in candidates:
        if n % c == 0:
            return c
    raise ValueError("unsupported token count")


def _to_colmajor(t, n_ctx, rows):
    lat = t[n_ctx:]
    lat = lat.reshape((rows, GRID_W) + lat.shape[1:]).swapaxes(0, 1).reshape(lat.shape)
    return jnp.concatenate([t[:n_ctx], lat], axis=0)


def _from_colmajor(t, n_ctx, rows):
    lat = t[n_ctx:]
    lat = lat.reshape((GRID_W, rows) + lat.shape[1:]).swapaxes(0, 1).reshape(lat.shape)
    return jnp.concatenate([t[:n_ctx], lat], axis=0)


def kernel(x, c, ctx, c_ctx, w_mod, b_mod, norm1_g, w_in, rw_mu, rw_w0, rw_w2, rw_a0, rw_a2, rw_g2, rw_kk, rw_ka, rw_rk, rw_ln_w, rw_ln_b, gla_a2, gla_ab, gla_norm_g, gdn_conv, gdn_a_log, gdn_dt_bias, gdn_norm_g, w_branch, w_out, norm2_g, w_mlp1, w_mlp2, final_g):
    bsz, n_lat, dm = x.shape
    assert bsz == 1 and dm == D_MODEL
    n_ctx = ctx.shape[1]
    n = n_ctx + n_lat
    rows = n_lat // GRID_W
    depth = w_in.shape[0]
    tb = _pick_block(n, n_ctx, (256, 128))
    tm = _pick_rows(n, (640, 512, 256, 128))
    tmlp = _pick_rows(n, (640, 512, 256, 128))

    cc = jnp.zeros((SUBLANES, D_MODEL), F32).at[0].set(c_ctx).at[1].set(c[0])
    mod_all = _modulation(cc, w_mod, b_mod)

    bd64 = _block_diag_ones(BRANCH_W, RW_HEAD)
    bd128 = _block_diag_ones(LANES, RW_HEAD)
    h = jnp.concatenate([ctx[0], x[0]], axis=0)

    for l in range(depth):
        mod = mod_all[l]
        o0 = 0
        w_rw = w_in[l][:, o0:o0 + RW_COLS].astype(BF16)
        o0 += RW_COLS
        w_gla = _pad_cols(w_in[l][:, o0:o0 + GLA_COLS], GLA_COLS_PAD).astype(BF16)
        o0 += GLA_COLS
        w_gdn = _pad_cols(w_in[l][:, o0:o0 + GDN_COLS], GDN_COLS_PAD).astype(BF16)
        o0 += GDN_COLS
        w_gate = w_in[l][:, o0:].astype(BF16)

        if l == 0:
            hn = _norm(h, norm1_g[l][None], mod, n_ctx, tb, BF16)
        hn_cm = _to_colmajor(hn, n_ctx, rows)
        p_rw = _project(hn, w_rw, tm, "proj_rwkv")
        p_gla = _project(hn_cm, w_gla, tm, "proj_gla")
        p_gdn = _project(hn, w_gdn, tm, "proj_gdn")

        mu = rw_mu[l]
        rw_wts = {
            "mu3": jnp.zeros((SUBLANES, RW_COLS), F32).at[0].set(mu[0]).at[1].set(1 - mu[0] - mu[1]).at[2].set(mu[1]),
            "w0": rw_w0[l], "w2p": _lora_pad(rw_w2[l], RW_DECAY_LORA),
            "a0": rw_a0[l], "a2p": _lora_pad(rw_a2[l], RW_A_LORA), "g2": rw_g2[l],
            "kka": jnp.zeros((SUBLANES, BRANCH_W), F32).at[0].set(rw_kk[l]).at[1].set(rw_ka[l])
                      .at[2].set(rw_rk[l].reshape(-1)),
            "bd64": bd64,
        }
        r, v, kk, lw, kd, bdir, g, bonus = _rw_prep(p_rw, rw_wts, n_ctx, tb)
        yf, yb = _rw_chunk(r, v, kk, lw, kd, bdir, n_ctx)
        ln = jnp.zeros((SUBLANES, BRANCH_W), F32).at[0].set(rw_ln_w[l]).at[1].set(rw_ln_b[l])

        of_gla, ob_gla = _gla_chunk(p_gla, _lora_pad(gla_a2[l], GLA_GATE_LORA), gla_ab[l], n_ctx)
        y_gla_cm = _gated_norm(of_gla, ob_gla, p_gla, 2, gla_norm_g[l][None], tb, "gla_post")
        y_gla = _from_colmajor(y_gla_cm, n_ctx, rows)

        conv_w = jnp.zeros((SUBLANES, 3 * BRANCH_W), F32).at[:GDN_CONV].set(gdn_conv[l])
        nh2 = 2 * GDN_HEADS
        gparams = jnp.zeros((SUBLANES, LANES), F32).at[0, :nh2].set(gdn_a_log[l].reshape(-1)) \
            .at[1, :nh2].set(gdn_dt_bias[l].reshape(-1))
        q_d, k_d, v_d, gb = _gdn_prep(p_gdn, conv_w, gparams, n_ctx, tb)
        of, ob = _gdn_chunk(q_d, k_d, v_d, gb, n_ctx)

        h = _merge(hn, yf, yb, bonus, g, ln, bd128, y_gla, of, ob, p_gdn, gdn_norm_g[l][None], h, w_gate,
                   w_branch[l].astype(BF16), w_out[l].astype(BF16), mod, n_ctx, tb)
        last = l == depth - 1
        g_next = final_g[None] if last else norm1_g[l + 1][None]
        mod_next = mod if last else mod_all[l + 1]
        res = _mlp(h, norm2_g[l][None], mod, w_mlp1[l].astype(BF16), w_mlp2[l].astype(BF16), g_next, mod_next,
                   n_ctx, tmlp, 1024, last)
        if last:
            out = res[0]
        else:
            h, hn = res

    return out[n_ctx:][None]
```

```python
import functools

import jax
import jax.numpy as jnp
from jax import lax
from jax.experimental import pallas as pl
from jax.experimental.pallas import tpu as pltpu

F32 = jnp.float32
BF16 = jnp.bfloat16

D_MODEL = 1024
GRID_W = 64
CHUNK = 64
BRANCH_W = D_MODEL // 2
NORM_EPS = 1e-6
RW_HEAD = 64
RW_HEADS = BRANCH_W // RW_HEAD
RW_PAIRS = RW_HEADS // 2
RW_DECAY_LORA = 64
RW_A_LORA = 64
RW_G_LORA = 128
RW_GN_EPS = 64e-5
GLA_HEADS = 4
GLA_DK = 64
GLA_DV = BRANCH_W // GLA_HEADS
GLA_GATE_LORA = 16
GLA_GATE_NORM = 16.0
GDN_HEADS = 4
GDN_HEAD = BRANCH_W // GDN_HEADS
GDN_CONV = 5
MLP_HIDDEN = 4 * D_MODEL
RW_COLS = 3 * BRANCH_W + 2 * RW_DECAY_LORA + 2 * RW_A_LORA + RW_G_LORA
GLA_COLS = 2 * GLA_HEADS * GLA_DK + 2 * BRANCH_W + 2 * GLA_GATE_LORA
GDN_COLS = 4 * BRANCH_W + 4 * GDN_HEADS

LANES = 128
SUBLANES = 8
GLA_COLS_PAD = -(-GLA_COLS // LANES) * LANES
GDN_COLS_PAD = -(-GDN_COLS // LANES) * LANES
VMEM_LIMIT_BYTES = 48 * 1024 * 1024

RW_CPS = 2
GLA_CPS = 2
GDN_CPS = 2

NN = (((1,), (0,)), ((), ()))
NT = (((1,), (1,)), ((), ()))
TN = (((0,), (0,)), ((), ()))


def _dg(a, b, dims=NN):
    return lax.dot_general(a, b, dims, preferred_element_type=F32)


def _dot1(a, b, dims=NN):
    return _dg(a.astype(BF16), b.astype(BF16), dims)


def _split2(a):
    hi = a.astype(BF16)
    lo = (a - hi.astype(F32)).astype(BF16)
    return hi, lo


def _dot3(a, b):
    ah, al = _split2(a)
    bh, bl = _split2(b)
    m = a.shape[0]
    both = _dg(jnp.concatenate([ah, al], axis=0), bh)
    return both[:m] + (both[m:] + _dg(ah, bl))


def _split3(x):
    h1 = x.astype(BF16)
    r1 = x - h1.astype(F32)
    h2 = r1.astype(BF16)
    h3 = (r1 - h2.astype(F32)).astype(BF16)
    return h1, h2, h3


def _dotc(c_bf16, x, dims=NN):
    h1, h2, h3 = _split3(x)
    return _dg(c_bf16, h1, dims) + (_dg(c_bf16, h2, dims) + _dg(c_bf16, h3, dims))


def _dotxc(x, c_bf16, dims=NN):
    h1, h2, h3 = _split3(x)
    return _dg(h1, c_bf16, dims) + (_dg(h2, c_bf16, dims) + _dg(h3, c_bf16, dims))


def _iota(shape, axis):
    return lax.broadcasted_iota(jnp.int32, shape, axis)


def _order_masks(n, rev):
    row = _iota((n, n), 0)
    col = _iota((n, n), 1)
    if rev:
        return col >= row, col > row
    return col <= row, col < row


def _pair_rows(x, first):
    return jnp.concatenate([jnp.where(first, x, 0.0), jnp.where(first, 0.0, x)], axis=0)


def _tri_inverse_pairs(a_list):
    n = a_list[0].shape[0]
    row = _iota((n, LANES), 0)
    col = _iota((n, LANES), 1) % n
    r2 = _iota((LANES, LANES), 0)
    c2 = _iota((LANES, LANES), 1)
    same_head = (r2 < n) == (c2 < n)

    def bdiag(m):
        return jnp.where(same_head, jnp.concatenate([m, m], axis=0), 0.0)

    eye = (row == col).astype(F32)
    t = [eye - jnp.where((row // 2) == (col // 2), a, 0.0) for a in a_list]
    b = 2
    while b < n:
        off = jnp.logical_and((row // (2 * b)) == (col // (2 * b)), (row // b) != (col // b))
        x = [_dot1(jnp.where(off, a, 0.0), bdiag(ti)) for a, ti in zip(a_list, t)]
        t = [ti - _dot1(ti, bdiag(xi)) for ti, xi in zip(t, x)]
        b *= 2
    return t


def _sigmoid(x):
    return 1.0 / (1.0 + jnp.exp(-x))


def _softplus(x):
    return jnp.maximum(x, 0.0) + jnp.log(1.0 + jnp.exp(-jnp.abs(x)))


def _silu(x):
    return x * _sigmoid(x)


def _params(sem):
    return pltpu.CompilerParams(dimension_semantics=sem, vmem_limit_bytes=VMEM_LIMIT_BYTES)


def _chunk_maps(n, n_ctx, cps):
    n_blocks = n // (CHUNK * cps)
    ctx_blocks = n_ctx // (CHUNK * cps)

    def fwd(i):
        return i

    def bwd(i):
        return jnp.where(i < ctx_blocks, ctx_blocks - 1 - i, n_blocks + ctx_blocks - 1 - i)

    return n_blocks, fwd, bwd


def _mod_kernel(cc_ref, w_ref, b_ref, o_ref):
    s = _silu(cc_ref[...])
    o_ref[...] = _dot3(s, w_ref[...]) + b_ref[...]


def _modulation(cc, w_mod, b_mod):
    depth = w_mod.shape[0]
    ncol = w_mod.shape[2] // D_MODEL
    return pl.pallas_call(
        _mod_kernel,
        grid=(depth, ncol),
        in_specs=[
            pl.BlockSpec((SUBLANES, D_MODEL), lambda l, j: (0, 0)),
            pl.BlockSpec((None, D_MODEL, D_MODEL), lambda l, j: (l, 0, j)),
            pl.BlockSpec((None, 1, D_MODEL), lambda l, j: (l, 0, j)),
        ],
        out_specs=pl.BlockSpec((None, SUBLANES, D_MODEL), lambda l, j: (l, 0, j)),
        out_shape=jax.ShapeDtypeStruct((depth, SUBLANES, w_mod.shape[2]), F32),
        compiler_params=_params(("parallel", "parallel")),
        name="modulation",
    )(cc, w_mod, b_mod.reshape(depth, 1, -1))


def _seg_vec(mod_ref, idx, row0, nrows, n_ctx):
    lo = idx * D_MODEL
    ctx = mod_ref[0:1, lo:lo + D_MODEL]
    lat = mod_ref[1:2, lo:lo + D_MODEL]
    is_ctx = (_iota((nrows, 1), 0) + row0) < n_ctx
    return jnp.where(is_ctx, ctx, lat)


def _rmsnorm_rows(x, g):
    ms = jnp.mean(x * x, axis=-1, keepdims=True)
    return x * lax.rsqrt(ms + NORM_EPS) * g


def _norm_kernel(x_ref, g_ref, mod_ref, o_ref, *, n_ctx, tb, shift_idx, scale_idx):
    row0 = pl.program_id(0) * tb
    xn = _rmsnorm_rows(x_ref[...], g_ref[...])
    if scale_idx is not None:
        xn = xn * (1.0 + _seg_vec(mod_ref, scale_idx, row0, tb, n_ctx)) \
            + _seg_vec(mod_ref, shift_idx, row0, tb, n_ctx)
    o_ref[...] = xn.astype(o_ref.dtype)


def _norm(x, g, mod, n_ctx, tb, out_dtype, shift_idx=0, scale_idx=1):
    n = x.shape[0]
    return pl.pallas_call(
        functools.partial(_norm_kernel, n_ctx=n_ctx, tb=tb, shift_idx=shift_idx, scale_idx=scale_idx),
        grid=(n // tb,),
        in_specs=[
            pl.BlockSpec((tb, D_MODEL), lambda i: (i, 0)),
            pl.BlockSpec((1, D_MODEL), lambda i: (0, 0)),
            pl.BlockSpec(mod.shape, lambda i: (0, 0)),
        ],
        out_specs=pl.BlockSpec((tb, D_MODEL), lambda i: (i, 0)),
        out_shape=jax.ShapeDtypeStruct((n, D_MODEL), out_dtype),
        compiler_params=_params(("parallel",)),
        name="adaln_norm",
    )(x, g, mod)


def _mm_kernel(a_ref, w_ref, o_ref):
    o_ref[...] = _dg(a_ref[...], w_ref[...])


def _project(a, w, tm, name):
    n, k = a.shape
    cols = w.shape[1]
    return pl.pallas_call(
        _mm_kernel,
        grid=(n // tm,),
        in_specs=[pl.BlockSpec((tm, k), lambda i: (i, 0)),
                  pl.BlockSpec((k, cols), lambda i: (0, 0))],
        out_specs=pl.BlockSpec((tm, cols), lambda i: (i, 0)),
        out_shape=jax.ShapeDtypeStruct((n, cols), F32),
        compiler_params=_params(("parallel",)),
        name=name,
    )(a, w)


def _fill_ext(ext_ref, cur_ref, prev_ref, next_ref, has_prev, has_next, tb):
    zeros = jnp.zeros(prev_ref.shape, F32)
    ext_ref[0:SUBLANES, :] = jnp.where(has_prev, prev_ref[...], zeros)
    ext_ref[SUBLANES:SUBLANES + tb, :] = cur_ref[...]
    ext_ref[SUBLANES + tb:SUBLANES + tb + SUBLANES, :] = jnp.where(has_next, next_ref[...], zeros)


def _halo_maps(tb, nblk):
    r = tb // SUBLANES

    def prev_map(i):
        return (jnp.maximum(i * r - 1, 0), 0)

    def next_map(i):
        return (jnp.minimum((i + 1) * r, nblk * r - 1), 0)

    return prev_map, next_map


def _rw_prep_kernel(cur_ref, prev_ref, next_ref, mu_ref, w0_ref, w2_ref, a0_ref, a2_ref, g2_ref,
                    kka_ref, bd_ref,
                    r_ref, v_ref, kk_ref, lw_ref, kd_ref, bdir_ref, g_ref, bonus_ref,
                    *, tb, ctx_blocks, nblk):
    i = pl.program_id(0)
    seg_first = jnp.logical_or(i == 0, i == ctx_blocks)
    seg_last = jnp.logical_or(i == ctx_blocks - 1, i == nblk - 1)
    x = cur_ref[...]
    row = _iota((tb, 1), 0)
    prev_row = jnp.where(seg_first, 0.0, prev_ref[SUBLANES - 1:SUBLANES, :])
    next_row = jnp.where(seg_last, 0.0, next_ref[0:1, :])
    x_prev = jnp.where(row == 0, prev_row, pltpu.roll(x, 1, axis=0))
    x_next = jnp.where(row == tb - 1, next_row, pltpu.roll(x, tb - 1, axis=0))
    xs = mu_ref[0:1, :] * x_prev + mu_ref[1:2, :] * x + mu_ref[2:3, :] * x_next
    bw = BRANCH_W
    r = xs[:, 0:bw]
    k = xs[:, bw:2 * bw]
    v = xs[:, 2 * bw:3 * bw]
    wl = xs[:, 3 * bw:3 * bw + LANES]
    al = xs[:, 3 * bw + LANES:3 * bw + 2 * LANES]
    gl = xs[:, 3 * bw + 2 * LANES:3 * bw + 3 * LANES]
    bd = bd_ref[...]
    k_k = kka_ref[0:1, :]
    k_a = kka_ref[1:2, :]
    r_k = kka_ref[2:3, :]

    kk0 = k * k_k
    kk = kk0 * lax.rsqrt(_dotxc(kk0 * kk0, bd) + 1e-12)
    g = _dot3(_sigmoid(gl), g2_ref[...])
    twl = jnp.tanh(wl)
    ksum = jnp.zeros_like(k)
    for z in range(2):
        wlog = w0_ref[z:z + 1, :] + _dot3(twl, w2_ref[z])
        w = -_softplus(-wlog) - 0.5
        lw = -jnp.exp(w)
        a = _sigmoid(a0_ref[z:z + 1, :] + _dot3(al, a2_ref[z]))
        kd = k * (1.0 + (a - 1.0) * k_a)
        bdir = kk * a
        ksum = ksum + kd
        for p in range(RW_PAIRS):
            sl = slice(p * LANES, (p + 1) * LANES)
            lw_ref[z, p] = lw[:, sl]
            kd_ref[z, p] = kd[:, sl]
            bdir_ref[z, p] = bdir[:, sl]
    bonus = _dotxc(r * ksum * r_k, bd) * v
    for p in range(RW_PAIRS):
        sl = slice(p * LANES, (p + 1) * LANES)
        r_ref[p] = r[:, sl]
        v_ref[p] = v[:, sl]
        kk_ref[p] = kk[:, sl]
        g_ref[p] = g[:, sl]
        bonus_ref[p] = bonus[:, sl]


def _rw_prep(p_rw, wts, n_ctx, tb):
    n = p_rw.shape[0]
    nblk = n // tb
    prev_map, next_map = _halo_maps(tb, nblk)
    full = lambda a: pl.BlockSpec(a.shape, lambda i: (0,) * a.ndim)
    pair_spec = pl.BlockSpec((RW_PAIRS, tb, LANES), lambda i: (0, i, 0))
    dir_spec = pl.BlockSpec((2, RW_PAIRS, tb, LANES), lambda i: (0, 0, i, 0))
    pair_shape = jax.ShapeDtypeStruct((RW_PAIRS, n, LANES), F32)
    dir_shape = jax.ShapeDtypeStruct((2, RW_PAIRS, n, LANES), F32)
    params = [wts["mu3"], wts["w0"], wts["w2p"], wts["a0"], wts["a2p"], wts["g2"], wts["kka"], wts["bd64"]]
    return pl.pallas_call(
        functools.partial(_rw_prep_kernel, tb=tb, ctx_blocks=n_ctx // tb, nblk=nblk),
        grid=(nblk,),
        in_specs=[pl.BlockSpec((tb, RW_COLS), lambda i: (i, 0)),
                  pl.BlockSpec((SUBLANES, RW_COLS), prev_map),
                  pl.BlockSpec((SUBLANES, RW_COLS), next_map)]
                 + [full(a) for a in params],
        out_specs=[pair_spec, pair_spec, pair_spec, dir_spec, dir_spec, dir_spec, pair_spec, pair_spec],
        out_shape=[pair_shape, pair_shape, pair_shape, dir_shape, dir_shape, dir_shape, pair_shape, pair_shape],
        compiler_params=_params(("parallel",)),
        name="rwkv_prep",
    )(p_rw, p_rw, p_rw, *params)


def _rw_chunk_group(items, m_ref):
    n = CHUNK
    prow = _iota((n, LANES), 0)
    pcol = _iota((n, LANES), 1) % n
    masks = {False: (pcol <= prow, pcol < prow), True: (pcol >= prow, pcol > prow)}
    head_a = _iota((n, LANES), 1) < RW_HEAD
    head_a2 = jnp.concatenate([head_a, head_a], axis=1)
    row = _iota((LANES, LANES), 0)
    col = _iota((LANES, LANES), 1)
    same_head = (row < RW_HEAD) == (col < RW_HEAD)

    for d in items:
        r, v, kk, lw, kd, bdir = d["tile"]
        incl, _ = _order_masks(n, d["rev"])
        cum = _dotc(incl.astype(BF16), lw)
        cum_end = cum[0:1, :] if d["rev"] else cum[n - 1:n, :]
        e_inv = jnp.exp(-cum)
        e_end = jnp.exp(cum_end - cum)
        d.update(v=v, kkd=kk * jnp.exp(cum - lw), kinv=kd * e_inv, binv=bdir * e_inv, rd=r * jnp.exp(cum),
                 kend=kd * e_end, bend=bdir * e_end, g_end=jnp.exp(cum_end))

    for d in items:
        incl2, strict2 = masks[d["rev"]]
        lhs = jnp.concatenate([d["kkd"], d["rd"]], axis=0)
        with_b = _dot1(lhs, _pair_rows(d["binv"], head_a), NT)
        with_k = _dot1(lhs, _pair_rows(d["kinv"], head_a), NT)
        d["a_kb"] = jnp.where(strict2, with_b[:n], 0.0)
        d["a_rb"] = jnp.where(incl2, with_b[n:], 0.0)
        d["a_kk"] = jnp.where(strict2, with_k[:n], 0.0)
        d["a_rk"] = jnp.where(incl2, with_k[n:], 0.0)
        d["v_rows"] = _pair_rows(d["v"], head_a).astype(BF16)
    t = _tri_inverse_pairs([d["a_kb"] for d in items])
    for d, ti in zip(items, t):
        akv = _dot1(d["a_kk"], d["v_rows"])
        d["uw"] = _dot1(ti, _pair_rows(jnp.concatenate([akv, d["kkd"]], axis=1), head_a2))
    for d in items:
        rb_uw = _dot1(d["a_rb"], _pair_rows(d["uw"], head_a2))
        d["y0"] = _dot1(d["a_rk"], d["v_rows"]) - rb_uw[:, :LANES]
        d["qp"] = d["rd"] - rb_uw[:, LANES:]
        b_uw = _dot1(d["bend"], d["uw"], TN)
        kv = _dot1(d["kend"], d["v"], TN)
        bw_p = jnp.where(head_a, b_uw[:n, LANES:], b_uw[n:, LANES:])
        d["p_mat"] = jnp.where(prow == pcol, jnp.broadcast_to(d["g_end"], (n, LANES)), 0.0) - bw_p
        d["z0"] = jnp.where(head_a, kv[:n] - b_uw[:n, :LANES], kv[n:] - b_uw[n:, :LANES])

    for z, p in sorted({(d["z"], d["p"]) for d in items}):
        mine = sorted((d for d in items if (d["z"], d["p"]) == (z, p)), key=lambda d: d["c"], reverse=bool(z))
        m = m_ref[z, p]
        for d in mine:
            m_bd = jnp.where(same_head, jnp.concatenate([m, m], axis=0), 0.0)
            qm_pm = _dot3(jnp.concatenate([d["qp"], d["p_mat"]], axis=0), m_bd)
            d["y_ref"][p, d["rows"], :] = d["y0"] + qm_pm[:n]
            m = qm_pm[n:] + d["z0"]
        m_ref[z, p] = m


def _rw_chunk_kernel(rf, vf, kkf, lwf, kdf, bf, rb, vb, kkb, lwb, kdb, bb, yf_ref, yb_ref, m_ref):
    @pl.when(pl.program_id(0) == 0)
    def _():
        m_ref[...] = jnp.zeros(m_ref.shape, F32)

    dirs = ((rf, vf, kkf, lwf, kdf, bf, yf_ref, False), (rb, vb, kkb, lwb, kdb, bb, yb_ref, True))
    items = []
    for z, (r, v, kk, lw, kd, b, y_ref, rev) in enumerate(dirs):
        for p in range(RW_PAIRS):
            for c in range(RW_CPS):
                rows = slice(c * CHUNK, (c + 1) * CHUNK)
                items.append(dict(z=z, p=p, c=c, rev=rev, rows=rows, y_ref=y_ref,
                                  tile=tuple(x[p, rows, :] for x in (r, v, kk, lw, kd, b))))
    _rw_chunk_group(items, m_ref)


def _rw_chunk(r, v, kk, lw, kd, bdir, n_ctx):
    n = r.shape[1]
    n_blocks, fwd, bwd = _chunk_maps(n, n_ctx, RW_CPS)
    rows = CHUNK * RW_CPS

    def shared(cm):
        return pl.BlockSpec((RW_PAIRS, rows, LANES), lambda i: (0, cm(i), 0))

    def per_dir(z, cm):
        return pl.BlockSpec((None, RW_PAIRS, rows, LANES), lambda i: (z, 0, cm(i), 0))

    in_specs = [shared(fwd), shared(fwd), shared(fwd), per_dir(0, fwd), per_dir(0, fwd), per_dir(0, fwd),
                shared(bwd), shared(bwd), shared(bwd), per_dir(1, bwd), per_dir(1, bwd), per_dir(1, bwd)]
    shape = jax.ShapeDtypeStruct((RW_PAIRS, n, LANES), F32)
    return pl.pallas_call(
        _rw_chunk_kernel,
        grid=(n_blocks,),
        in_specs=in_specs,
        out_specs=[shared(fwd), shared(bwd)],
        out_shape=[shape, shape],
        scratch_shapes=[pltpu.VMEM((2, RW_PAIRS, CHUNK, LANES), F32)],
        compiler_params=_params(("arbitrary",)),
        name="rwkv_chunk",
    )(r, v, kk, lw, kd, bdir, r, v, kk, lw, kd, bdir)


def _gla_chunk_kernel(qf, kf, vf, alf, qb, kb, vb, alb, a2_ref, ab_ref, of_ref, ob_ref, s_ref):
    @pl.when(pl.program_id(0) == 0)
    def _():
        s_ref[...] = jnp.zeros(s_ref.shape, F32)

    n = CHUNK
    head_a = _iota((n, LANES), 1) < GLA_DK
    top = _iota((LANES, LANES), 0) < GLA_DK
    items = []
    for z, (q_ref, k_ref, v_ref, al_ref, o_ref) in enumerate(((qf, kf, vf, alf, of_ref), (qb, kb, vb, alb, ob_ref))):
        rev = bool(z)
        incl, _ = _order_masks(n, rev)
        for c in range(GLA_CPS):
            rows = slice(c * n, (c + 1) * n)
            zlog = _dot3(al_ref[rows, :], a2_ref[z]) + ab_ref[z:z + 1, :]
            log_a = (jnp.minimum(zlog, 0.0) - jnp.log(1.0 + jnp.exp(-jnp.abs(zlog)))) * (1.0 / GLA_GATE_NORM)
            bcum = _dotc(incl.astype(BF16), log_a)
            b_end = bcum[0:1, :] if rev else bcum[n - 1:n, :]
            q_dec = q_ref[rows, :] * (GLA_DK ** -0.5) * jnp.exp(bcum)
            k_inv = k_ref[rows, :] * jnp.exp(-bcum)
            k_end = k_ref[rows, :] * jnp.exp(b_end - bcum)
            dec_end = jnp.exp(b_end)
            for p in range(GLA_HEADS // 2):
                sl = slice(p * LANES, (p + 1) * LANES)
                items.append(dict(z=z, c=c, p=p, incl=incl, rows=rows, o_ref=o_ref, qd=q_dec[:, sl], ki=k_inv[:, sl],
                                  ke=k_end[:, sl], dec=dec_end[:, sl],
                                  v=[v_ref[rows, (2 * p + h) * GLA_DV:(2 * p + h + 1) * GLA_DV] for h in range(2)]))
    for d in items:
        d["qd_h"] = [jnp.where(head_a if h == 0 else jnp.logical_not(head_a), d["qd"], 0.0) for h in range(2)]
        d["att"] = [jnp.where(d["incl"], _dot1(d["qd_h"][h], d["ki"], NT), 0.0) for h in range(2)]
    for d in items:
        d["intra"] = [_dot1(d["att"][h], d["v"][h]) for h in range(2)]
        upd = [_dot1(d["ke"], d["v"][h], TN) for h in range(2)]
        d["upd"] = jnp.where(top, upd[0], upd[1])
        d["dec_col"] = jnp.transpose(jnp.broadcast_to(d["dec"], (LANES, LANES)))
    for z in range(2):
        order = range(GLA_CPS - 1, -1, -1) if z else range(GLA_CPS)
        for p in range(GLA_HEADS // 2):
            s = s_ref[z, p]
            for c in order:
                d = next(x for x in items if (x["z"], x["c"], x["p"]) == (z, c, p))
                inter = _dot3(jnp.concatenate(d["qd_h"], axis=0), s)
                for h in range(2):
                    hh = 2 * p + h
                    d["o_ref"][d["rows"], hh * GLA_DV:(hh + 1) * GLA_DV] = d["intra"][h] + inter[h * n:(h + 1) * n]
                s = d["dec_col"] * s + d["upd"]
            s_ref[z, p] = s


def _gla_chunk(p_gla, a2p, ab, n_ctx):
    n = p_gla.shape[0]
    n_blocks, fwd, bwd = _chunk_maps(n, n_ctx, GLA_CPS)
    hk = GLA_HEADS * GLA_DK
    rows = CHUNK * GLA_CPS

    def specs(cm):
        return [pl.BlockSpec((rows, hk), lambda i: (cm(i), 0)),
                pl.BlockSpec((rows, hk), lambda i: (cm(i), 1)),
                pl.BlockSpec((rows, BRANCH_W), lambda i: (cm(i), 1)),
                pl.BlockSpec((rows, LANES), lambda i: (cm(i), (2 * hk + 2 * BRANCH_W) // LANES))]

    full = lambda a: pl.BlockSpec(a.shape, lambda i: (0,) * a.ndim)
    shape = jax.ShapeDtypeStruct((n, BRANCH_W), F32)
    return pl.pallas_call(
        _gla_chunk_kernel,
        grid=(n_blocks,),
        in_specs=specs(fwd) + specs(bwd) + [full(a2p), full(ab)],
        out_specs=[pl.BlockSpec((rows, BRANCH_W), lambda i: (fwd(i), 0)),
                   pl.BlockSpec((rows, BRANCH_W), lambda i: (bwd(i), 0))],
        out_shape=[shape, shape],
        scratch_shapes=[pltpu.VMEM((2, GLA_HEADS // 2, LANES, LANES), F32)],
        compiler_params=_params(("arbitrary",)),
        name="gla_chunk",
    )(p_gla, p_gla, p_gla, p_gla, p_gla, p_gla, p_gla, p_gla, a2p, ab)


def _gated_norm_kernel(of_ref, ob_ref, gate_ref, g_ref, o_ref):
    for h in range(BRANCH_W // LANES):
        sl = slice(h * LANES, (h + 1) * LANES)
        o = of_ref[:, sl] + ob_ref[:, sl]
        ms = jnp.mean(o * o, axis=-1, keepdims=True)
        o_ref[:, sl] = (o * lax.rsqrt(ms + NORM_EPS) * g_ref[...] * _silu(gate_ref[:, sl])).astype(o_ref.dtype)


def _gated_norm(of, ob, gate_src, gate_block, norm_g, tb, name):
    n = of.shape[0]
    spec = pl.BlockSpec((tb, BRANCH_W), lambda i: (i, 0))
    return pl.pallas_call(
        _gated_norm_kernel,
        grid=(n // tb,),
        in_specs=[spec, spec, pl.BlockSpec((tb, BRANCH_W), lambda i: (i, gate_block)),
                  pl.BlockSpec((1, LANES), lambda i: (0, 0))],
        out_specs=spec,
        out_shape=jax.ShapeDtypeStruct((n, BRANCH_W), BF16),
        compiler_params=_params(("parallel",)),
        name=name,
    )(of, ob, gate_src, norm_g)


def _gdn_prep_kernel(cur_ref, prev_ref, next_ref, sc_ref, cw_ref, gp_ref,
                     q_ref, k_ref, v_ref, gb_ref, ext_ref, *, tb, ctx_blocks, nblk):
    i = pl.program_id(0)
    seg_first = jnp.logical_or(i == 0, i == ctx_blocks)
    seg_last = jnp.logical_or(i == ctx_blocks - 1, i == nblk - 1)
    _fill_ext(ext_ref, cur_ref, prev_ref, next_ref, jnp.logical_not(seg_first),
              jnp.logical_not(seg_last), tb)
    pad = GDN_CONV // 2
    acc = cw_ref[0:1, :] * ext_ref[pl.ds(SUBLANES - pad, tb), :]
    for t in range(1, GDN_CONV):
        acc = acc + cw_ref[t:t + 1, :] * ext_ref[pl.ds(SUBLANES - pad + t, tb), :]
    qkv = _silu(acc)
    for h in range(GDN_HEADS):
        for j, ref in enumerate((q_ref, k_ref, v_ref)):
            sl = slice(j * BRANCH_W + h * GDN_HEAD, j * BRANCH_W + (h + 1) * GDN_HEAD)
            osl = slice(h * GDN_HEAD, (h + 1) * GDN_HEAD)
            x = qkv[:, sl]
            if j < 2:
                x = x * lax.rsqrt(jnp.sum(x * x, axis=-1, keepdims=True) + 1e-12)
            if j == 0:
                x = x * (GDN_HEAD ** -0.5)
            ref[:, osl] = x
    sc = sc_ref[...]
    lane = _iota(sc.shape, 1)
    nh2 = 2 * GDN_HEADS
    g = -jnp.exp(gp_ref[0:1, :]) * _softplus(sc + gp_ref[1:2, :])
    beta = _sigmoid(sc)
    gb_ref[...] = jnp.where(lane < nh2, g, jnp.where(lane < 2 * nh2, beta, 0.0))


def _gdn_prep(p_gdn, conv_w, gparams, n_ctx, tb):
    n = p_gdn.shape[0]
    nblk = n // tb
    qkv_cols = 3 * BRANCH_W
    prev_map, next_map = _halo_maps(tb, nblk)
    sc_block = (4 * BRANCH_W) // LANES
    full = lambda a: pl.BlockSpec(a.shape, lambda i: (0,) * a.ndim)
    shape = jax.ShapeDtypeStruct((n, BRANCH_W), F32)
    spec = pl.BlockSpec((tb, BRANCH_W), lambda i: (i, 0))
    return pl.pallas_call(
        functools.partial(_gdn_prep_kernel, tb=tb, ctx_blocks=n_ctx // tb, nblk=nblk),
        grid=(nblk,),
        in_specs=[pl.BlockSpec((tb, qkv_cols), lambda i: (i, 0)),
                  pl.BlockSpec((SUBLANES, qkv_cols), prev_map),
                  pl.BlockSpec((SUBLANES, qkv_cols), next_map),
                  pl.BlockSpec((tb, LANES), lambda i: (i, sc_block)),
                  full(conv_w), full(gparams)],
        out_specs=[spec, spec, spec, pl.BlockSpec((tb, LANES), lambda i: (i, 0))],
        out_shape=[shape, shape, shape, jax.ShapeDtypeStruct((n, LANES), F32)],
        scratch_shapes=[pltpu.VMEM((tb + 2 * SUBLANES, qkv_cols), F32)],
        compiler_params=_params(("parallel",)),
        name="gdn_prep",
    )(p_gdn, p_gdn, p_gdn, p_gdn, conv_w, gparams)


def _gdn_chunk_kernel(qf, kf, vf, gbf, qb, kb, vb, gbb, of_ref, ob_ref, s_ref):
    @pl.when(pl.program_id(0) == 0)
    def _():
        s_ref[...] = jnp.zeros(s_ref.shape, F32)

    n = CHUNK
    w2 = 2 * GDN_HEAD
    lane = _iota((n, LANES), 1)
    head_a = lane < n
    first2 = _iota((n, w2), 1) < GDN_HEAD
    prow = _iota((n, LANES), 0)
    pcol = lane % n
    diag = _iota((LANES, LANES), 0) == _iota((LANES, LANES), 1)
    ones = jnp.ones((n, LANES), BF16)
    blocks = []
    for z, (q_ref, k_ref, v_ref, gb_ref, o_ref) in enumerate(((qf, kf, vf, gbf, of_ref), (qb, kb, vb, gbb, ob_ref))):
        rev = bool(z)
        incl_sq, _ = _order_masks(n, rev)
        for c in range(GDN_CPS):
            rows = slice(c * n, (c + 1) * n)
            blocks.append(dict(z=z, c=c, rev=rev, rows=rows, refs=(q_ref, k_ref, v_ref, o_ref), gb=gb_ref[rows, :],
                               tri=incl_sq.astype(BF16)))
    for b in blocks:
        b["gam_all"] = _dotc(b["tri"], b["gb"])
    for b in blocks:
        b["gam_parts"] = _split3(b["gam_all"])
    items = []
    for b in blocks:
        z, rev, rows, gb, gam_all = b["z"], b["rev"], b["rows"], b["gb"], b["gam_all"]
        q_ref, k_ref, v_ref, o_ref = b["refs"]
        incl = (pcol >= prow) if rev else (pcol <= prow)
        strict = (pcol > prow) if rev else (pcol < prow)
        for hp in range(GDN_HEADS // 2):
            sl = slice(hp * w2, (hp + 1) * w2)
            js = [z * GDN_HEADS + 2 * hp + i for i in range(2)]
            gam_h = [jnp.broadcast_to(gam_all[:, j:j + 1], (n, LANES)) for j in js]
            beta_h = [jnp.broadcast_to(gb[:, 2 * GDN_HEADS + j:2 * GDN_HEADS + j + 1], (n, LANES)) for j in js]
            pick = [jnp.concatenate([jnp.where(lane == js[0], part, 0.0).astype(BF16),
                                     jnp.where(lane == js[1], part, 0.0).astype(BF16)], axis=0)
                    for part in b["gam_parts"]]
            gam2 = jnp.concatenate(gam_h, axis=1)
            items.append(dict(z=z, c=b["c"], hp=hp, rows=rows, o_ref=o_ref, strict=strict, incl=incl,
                              pick=pick, gam_col=jnp.where(head_a, gam_h[0], gam_h[1]),
                              q=q_ref[rows, sl], k=k_ref[rows, sl], v=v_ref[rows, sl], gam2=gam2,
                              gam_end2=gam2[0:1, :] if rev else gam2[n - 1:n, :],
                              beta2=jnp.concatenate(beta_h, axis=1),
                              beta_p=jnp.where(head_a, beta_h[0], beta_h[1])))
    for d in items:
        d["gam_row"] = sum(_dg(ones, pk, NT) for pk in d["pick"])
    for d in items:
        d["decay"] = jnp.where(d["incl"], jnp.exp(jnp.where(d["incl"], d["gam_col"] - d["gam_row"], 0.0)), 0.0)

    for d in items:
        d["e_gam2"] = jnp.exp(d["gam2"])
        kq = _dot1(jnp.concatenate([d["k"], d["q"]], axis=0), _pair_rows(d["k"], first2), NT)
        d["a"] = jnp.where(d["strict"], d["beta_p"] * kq[:n] * d["decay"], 0.0)
        d["qk"] = kq[n:] * d["decay"]
    t = _tri_inverse_pairs([d["a"] for d in items])

    def heads_of(m):
        return m[:, :n], pltpu.roll(m, n, axis=1)[:, :n]

    for d, ti in zip(items, t):
        bv = d["beta2"] * d["v"]
        bek = d["beta2"] * d["e_gam2"] * d["k"]
        d["uw"] = []
        for i, t_h in enumerate(heads_of(ti)):
            hs = slice(i * GDN_HEAD, (i + 1) * GDN_HEAD)
            d["uw"].append(_dot1(t_h, jnp.concatenate([bv[:, hs], bek[:, hs]], axis=1)))
    for d in items:
        k_end = d["k"] * jnp.exp(d["gam_end2"] - d["gam2"])
        dec_end = jnp.exp(d["gam_end2"])
        q_dec = d["q"] * d["e_gam2"]
        d["p"], d["z0"], d["y0"], d["qp"] = [], [], [], []
        for i, qk_h in enumerate(heads_of(d["qk"])):
            hs = slice(i * GDN_HEAD, (i + 1) * GDN_HEAD)
            qx = _dot1(qk_h, d["uw"][i])
            d["y0"].append(qx[:, :LANES])
            d["qp"].append(q_dec[:, hs] - qx[:, LANES:])
            ke_uw = _dot1(k_end[:, hs], d["uw"][i], TN)
            d["p"].append(jnp.where(diag, jnp.broadcast_to(dec_end[:, hs], (LANES, LANES)), 0.0) - ke_uw[:, LANES:])
            d["z0"].append(ke_uw[:, :LANES])
    for z in range(2):
        order = range(GDN_CPS - 1, -1, -1) if z else range(GDN_CPS)
        for hp in range(GDN_HEADS // 2):
            for i in range(2):
                h = 2 * hp + i
                s = s_ref[z, h]
                for c in order:
                    d = next(x for x in items if (x["z"], x["hp"], x["c"]) == (z, hp, c))
                    qs_ps = _dot3(jnp.concatenate([d["qp"][i], d["p"][i]], axis=0), s)
                    d["o_ref"][d["rows"], h * GDN_HEAD:(h + 1) * GDN_HEAD] = d["y0"][i] + qs_ps[:n]
                    s = qs_ps[n:] + d["z0"][i]
                s_ref[z, h] = s


def _gdn_chunk(q, k, v, gb, n_ctx):
    n = q.shape[0]
    n_blocks, fwd, bwd = _chunk_maps(n, n_ctx, GDN_CPS)
    rows = CHUNK * GDN_CPS

    def specs(cm):
        wide = pl.BlockSpec((rows, BRANCH_W), lambda i: (cm(i), 0))
        return [wide, wide, wide, pl.BlockSpec((rows, LANES), lambda i: (cm(i), 0))]

    shape = jax.ShapeDtypeStruct((n, BRANCH_W), F32)
    return pl.pallas_call(
        _gdn_chunk_kernel,
        grid=(n_blocks,),
        in_specs=specs(fwd) + specs(bwd),
        out_specs=[pl.BlockSpec((rows, BRANCH_W), lambda i: (fwd(i), 0)),
                   pl.BlockSpec((rows, BRANCH_W), lambda i: (bwd(i), 0))],
        out_shape=[shape, shape],
        scratch_shapes=[pltpu.VMEM((2, GDN_HEADS, GDN_HEAD, GDN_HEAD), F32)],
        compiler_params=_params(("arbitrary",)),
        name="gdn_chunk",
    )(q, k, v, gb, q, k, v, gb)


def _merge_kernel(hn_ref, yf_ref, yb_ref, bonus_ref, g_ref, ln_ref, bd_ref, ygla_ref, of_ref, ob_ref, zg_ref,
                  gng_ref, h_ref, wg_ref, wb_ref, wo_ref, mod_ref, o_ref, *, n_ctx, tb):
    hn = hn_ref[...]
    bd = bd_ref[...]
    inv_n = 1.0 / RW_HEAD
    rw = jnp.zeros((tb, D_MODEL), F32)
    for p in range(RW_PAIRS):
        sl = slice(p * LANES, (p + 1) * LANES)
        y = yf_ref[p] + yb_ref[p]
        mu = _dotxc(y, bd) * inv_n
        d = y - mu
        var = _dotxc(d * d, bd) * inv_n
        yn = d * lax.rsqrt(var + RW_GN_EPS) * ln_ref[0:1, sl] + ln_ref[1:2, sl]
        y_rw = (yn + bonus_ref[p]) * g_ref[p]
        rw = rw + _dg(y_rw.astype(BF16), wb_ref[0, sl, :])
    gdn = jnp.zeros((tb, D_MODEL), F32)
    for hd in range(GDN_HEADS):
        sl = slice(hd * GDN_HEAD, (hd + 1) * GDN_HEAD)
        o = of_ref[:, sl] + ob_ref[:, sl]
        ms = jnp.mean(o * o, axis=-1, keepdims=True)
        y_gdn = o * lax.rsqrt(ms + NORM_EPS) * gng_ref[...] * _silu(zg_ref[:, sl])
        gdn = gdn + _dg(y_gdn.astype(BF16), wb_ref[2, sl, :])
    gla = _dg(ygla_ref[...], wb_ref[1])
    merged = jnp.zeros((tb, D_MODEL), F32)
    for g, branch in enumerate((rw, gla, gdn)):
        gate = _sigmoid(_dg(hn, wg_ref[:, g * D_MODEL:(g + 1) * D_MODEL]))
        merged = merged + branch * gate
    mix = _dg(merged.astype(BF16), wo_ref[...])
    gate_vec = _seg_vec(mod_ref, 2, pl.program_id(0) * tb, tb, n_ctx)
    o_ref[...] = h_ref[...] + mix * gate_vec


def _merge(hn, yf, yb, bonus, g, ln, bd128, y_gla, of, ob, p_gdn, gdn_norm_g, h, wg, wb, wo, mod, n_ctx, tb):
    n = h.shape[0]
    full = lambda a: pl.BlockSpec(a.shape, lambda i: (0,) * a.ndim)
    wide = pl.BlockSpec((tb, D_MODEL), lambda i: (i, 0))
    half = pl.BlockSpec((tb, BRANCH_W), lambda i: (i, 0))
    pair = pl.BlockSpec((RW_PAIRS, tb, LANES), lambda i: (0, i, 0))
    zg = pl.BlockSpec((tb, BRANCH_W), lambda i: (i, 3))
    return pl.pallas_call(
        functools.partial(_merge_kernel, n_ctx=n_ctx, tb=tb),
        grid=(n // tb,),
        in_specs=[wide, pair, pair, pair, pair, full(ln), full(bd128), half, half, half, zg, full(gdn_norm_g),
                  wide, full(wg), full(wb), full(wo), full(mod)],
        out_specs=wide,
        out_shape=jax.ShapeDtypeStruct((n, D_MODEL), F32),
        compiler_params=_params(("parallel",)),
        name="merge_out",
    )(hn, yf, yb, bonus, g, ln, bd128, y_gla, of, ob, p_gdn, gdn_norm_g, h, wg, wb, wo, mod)


def _mlp_kernel(h_ref, g_ref, mod_ref, w1_ref, w2_ref, gn_ref, modn_ref, *rest, n_ctx, tb, nj, last):
    if last:
        o_ref, hn_ref, acc_ref = rest
    else:
        o_ref, hnext_ref, hn_ref, acc_ref = rest
    j = pl.program_id(1)
    row0 = pl.program_id(0) * tb

    @pl.when(j == 0)
    def _():
        xn = _rmsnorm_rows(h_ref[...], g_ref[...])
        xn = xn * (1.0 + _seg_vec(mod_ref, 4, row0, tb, n_ctx)) + _seg_vec(mod_ref, 3, row0, tb, n_ctx)
        hn_ref[...] = xn.astype(BF16)
        acc_ref[...] = jnp.zeros(acc_ref.shape, F32)

    hid = jnp.maximum(_dg(hn_ref[...], w1_ref[...]), 0.0)
    acc_ref[...] += _dg((hid * hid).astype(BF16), w2_ref[...])

    @pl.when(j == nj - 1)
    def _():
        h_new = h_ref[...] + acc_ref[...] * _seg_vec(mod_ref, 5, row0, tb, n_ctx)
        nxt = _rmsnorm_rows(h_new, gn_ref[...])
        if last:
            o_ref[...] = nxt
        else:
            o_ref[...] = h_new
            nxt = nxt * (1.0 + _seg_vec(modn_ref, 1, row0, tb, n_ctx)) + _seg_vec(modn_ref, 0, row0, tb, n_ctx)
            hnext_ref[...] = nxt.astype(BF16)


def _mlp(h, g, mod, w1, w2, g_next, mod_next, n_ctx, tb, hb, last):
    n = h.shape[0]
    nj = MLP_HIDDEN // hb
    wide = pl.BlockSpec((tb, D_MODEL), lambda i, j: (i, 0))
    vec = pl.BlockSpec((1, D_MODEL), lambda i, j: (0, 0))
    out_specs = [wide] if last else [wide, wide]
    out_shape = [jax.ShapeDtypeStruct((n, D_MODEL), F32)]
    if not last:
        out_shape.append(jax.ShapeDtypeStruct((n, D_MODEL), BF16))
    return pl.pallas_call(
        functools.partial(_mlp_kernel, n_ctx=n_ctx, tb=tb, nj=nj, last=last),
        grid=(n // tb, nj),
        in_specs=[wide, vec, pl.BlockSpec(mod.shape, lambda i, j: (0, 0)),
                  pl.BlockSpec((D_MODEL, hb), lambda i, j: (0, j)),
                  pl.BlockSpec((hb, D_MODEL), lambda i, j: (j, 0)),
                  vec, pl.BlockSpec(mod_next.shape, lambda i, j: (0, 0))],
        out_specs=out_specs,
        out_shape=out_shape,
        scratch_shapes=[pltpu.VMEM((tb, D_MODEL), BF16), pltpu.VMEM((tb, D_MODEL), F32)],
        compiler_params=_params(("parallel", "arbitrary")),
        name="mlp",
    )(h, g, mod, w1, w2, g_next, mod_next)


def _block_diag_ones(n, blk):
    r = jnp.arange(n) // blk
    return (r[:, None] == r[None, :]).astype(BF16)


def _pad_cols(w, cols):
    return jnp.pad(w, ((0, 0), (0, cols - w.shape[1])))


def _lora_pad(w2, lora):
    out = jnp.zeros((2, LANES, w2.shape[2]), F32)
    for z in range(2):
        out = out.at[z, z * lora:(z + 1) * lora].set(w2[z])
    return out


def _pick_block(n, n_ctx, candidates):
    for c in candidates:
        if n % c == 0 and n_ctx % c == 0:
            return c
    raise ValueError("token counts must be multiples of the chunk length")


def _pick_rows(n, candidates):
    for c in candidates:
        if n % c == 0:
            return c
    raise ValueError("unsupported token count")


def _to_colmajor(t, n_ctx, rows):
    lat = t[n_ctx:]
    lat = lat.reshape((rows, GRID_W) + lat.shape[1:]).swapaxes(0, 1).reshape(lat.shape)
    return jnp.concatenate([t[:n_ctx], lat], axis=0)


def _from_colmajor(t, n_ctx, rows):
    lat = t[n_ctx:]
    lat = lat.reshape((GRID_W, rows) + lat.shape[1:]).swapaxes(0, 1).reshape(lat.shape)
    return jnp.concatenate([t[:n_ctx], lat], axis=0)


def kernel(x, c, ctx, c_ctx, w_mod, b_mod, norm1_g, w_in, rw_mu, rw_w0, rw_w2, rw_a0, rw_a2, rw_g2, rw_kk, rw_ka, rw_rk, rw_ln_w, rw_ln_b, gla_a2, gla_ab, gla_norm_g, gdn_conv, gdn_a_log, gdn_dt_bias, gdn_norm_g, w_branch, w_out, norm2_g, w_mlp1, w_mlp2, final_g):
    bsz, n_lat, dm = x.shape
    assert bsz == 1 and dm == D_MODEL
    n_ctx = ctx.shape[1]
    n = n_ctx + n_lat
    rows = n_lat // GRID_W
    depth = w_in.shape[0]
    tb = _pick_block(n, n_ctx, (256, 128))
    tm = _pick_rows(n, (640, 512, 256, 128))
    tmlp = _pick_rows(n, (640, 512, 256, 128))

    cc = jnp.zeros((SUBLANES, D_MODEL), F32).at[0].set(c_ctx).at[1].set(c[0])
    mod_all = _modulation(cc, w_mod, b_mod)

    bd64 = _block_diag_ones(BRANCH_W, RW_HEAD)
    bd128 = _block_diag_ones(LANES, RW_HEAD)
    h = jnp.concatenate([ctx[0], x[0]], axis=0)
    w_in_bf = w_in.astype(BF16)

    for l in range(depth):
        mod = mod_all[l]
        o0 = 0
        w_l = w_in_bf[l]
        w_rw = w_l[:, o0:o0 + RW_COLS]
        o0 += RW_COLS
        w_gla = _pad_cols(w_l[:, o0:o0 + GLA_COLS], GLA_COLS_PAD)
        o0 += GLA_COLS
        w_gdn = _pad_cols(w_l[:, o0:o0 + GDN_COLS], GDN_COLS_PAD)
        o0 += GDN_COLS
        w_gate = w_l[:, o0:]

        if l == 0:
            hn = _norm(h, norm1_g[l][None], mod, n_ctx, tb, BF16)
        hn_cm = _to_colmajor(hn, n_ctx, rows)
        p_rw = _project(hn, w_rw, tm, "proj_rwkv")
        p_gla = _project(hn_cm, w_gla, tm, "proj_gla")
        p_gdn = _project(hn, w_gdn, tm, "proj_gdn")

        mu = rw_mu[l]
        rw_wts = {
            "mu3": jnp.zeros((SUBLANES, RW_COLS), F32).at[0].set(mu[0]).at[1].set(1 - mu[0] - mu[1]).at[2].set(mu[1]),
            "w0": rw_w0[l], "w2p": _lora_pad(rw_w2[l], RW_DECAY_LORA),
            "a0": rw_a0[l], "a2p": _lora_pad(rw_a2[l], RW_A_LORA), "g2": rw_g2[l],
            "kka": jnp.zeros((SUBLANES, BRANCH_W), F32).at[0].set(rw_kk[l]).at[1].set(rw_ka[l])
                      .at[2].set(rw_rk[l].reshape(-1)),
            "bd64": bd64,
        }
        r, v, kk, lw, kd, bdir, g, bonus = _rw_prep(p_rw, rw_wts, n_ctx, tb)
        yf, yb = _rw_chunk(r, v, kk, lw, kd, bdir, n_ctx)
        ln = jnp.zeros((SUBLANES, BRANCH_W), F32).at[0].set(rw_ln_w[l]).at[1].set(rw_ln_b[l])

        of_gla, ob_gla = _gla_chunk(p_gla, _lora_pad(gla_a2[l], GLA_GATE_LORA), gla_ab[l], n_ctx)
        y_gla_cm = _gated_norm(of_gla, ob_gla, p_gla, 2, gla_norm_g[l][None], tb, "gla_post")
        y_gla = _from_colmajor(y_gla_cm, n_ctx, rows)

        conv_w = jnp.zeros((SUBLANES, 3 * BRANCH_W), F32).at[:GDN_CONV].set(gdn_conv[l])
        nh2 = 2 * GDN_HEADS
        gparams = jnp.zeros((SUBLANES, LANES), F32).at[0, :nh2].set(gdn_a_log[l].reshape(-1)) \
            .at[1, :nh2].set(gdn_dt_bias[l].reshape(-1))
        q_d, k_d, v_d, gb = _gdn_prep(p_gdn, conv_w, gparams, n_ctx, tb)
        of, ob = _gdn_chunk(q_d, k_d, v_d, gb, n_ctx)

        h = _merge(hn, yf, yb, bonus, g, ln, bd128, y_gla, of, ob, p_gdn, gdn_norm_g[l][None], h, w_gate,
                   w_branch[l].astype(BF16), w_out[l].astype(BF16), mod, n_ctx, tb)
        last = l == depth - 1
        g_next = final_g[None] if last else norm1_g[l + 1][None]
        mod_next = mod if last else mod_all[l + 1]
        res = _mlp(h, norm2_g[l][None], mod, w_mlp1[l].astype(BF16), w_mlp2[l].astype(BF16), g_next, mod_next,
                   n_ctx, tmlp, MLP_HIDDEN // 2, last)
        if last:
            out = res[0]
        else:
            h, hn = res

    return out[n_ctx:][None]
```

```python
import functools

import jax
import jax.numpy as jnp
from jax import lax
from jax.experimental import pallas as pl
from jax.experimental.pallas import tpu as pltpu

F32 = jnp.float32
BF16 = jnp.bfloat16

D_MODEL = 1024
GRID_W = 64
CHUNK = 64
BRANCH_W = D_MODEL // 2
NORM_EPS = 1e-6
RW_HEAD = 64
RW_HEADS = BRANCH_W // RW_HEAD
RW_PAIRS = RW_HEADS // 2
RW_DECAY_LORA = 64
RW_A_LORA = 64
RW_G_LORA = 128
RW_GN_EPS = 64e-5
GLA_HEADS = 4
GLA_DK = 64
GLA_DV = BRANCH_W // GLA_HEADS
GLA_GATE_LORA = 16
GLA_GATE_NORM = 16.0
GDN_HEADS = 4
GDN_HEAD = BRANCH_W // GDN_HEADS
GDN_CONV = 5
MLP_HIDDEN = 4 * D_MODEL
RW_COLS = 3 * BRANCH_W + 2 * RW_DECAY_LORA + 2 * RW_A_LORA + RW_G_LORA
GLA_COLS = 2 * GLA_HEADS * GLA_DK + 2 * BRANCH_W + 2 * GLA_GATE_LORA
GDN_COLS = 4 * BRANCH_W + 4 * GDN_HEADS

LANES = 128
SUBLANES = 8
GLA_COLS_PAD = -(-GLA_COLS // LANES) * LANES
GDN_COLS_PAD = -(-GDN_COLS // LANES) * LANES
VMEM_LIMIT_BYTES = 48 * 1024 * 1024

RW_CPS = 2
GLA_CPS = 2
GDN_CPS = 2

NN = (((1,), (0,)), ((), ()))
NT = (((1,), (1,)), ((), ()))
TN = (((0,), (0,)), ((), ()))


def _dg(a, b, dims=NN):
    return lax.dot_general(a, b, dims, preferred_element_type=F32)


def _dot1(a, b, dims=NN):
    return _dg(a.astype(BF16), b.astype(BF16), dims)


def _split2(a):
    hi = a.astype(BF16)
    lo = (a - hi.astype(F32)).astype(BF16)
    return hi, lo


def _dot3(a, b):
    ah, al = _split2(a)
    bh, bl = _split2(b)
    m = a.shape[0]
    both = _dg(jnp.concatenate([ah, al], axis=0), bh)
    return both[:m] + (both[m:] + _dg(ah, bl))


def _split3(x):
    h1 = x.astype(BF16)
    r1 = x - h1.astype(F32)
    h2 = r1.astype(BF16)
    h3 = (r1 - h2.astype(F32)).astype(BF16)
    return h1, h2, h3


def _dotc(c_bf16, x, dims=NN):
    h1, h2, h3 = _split3(x)
    return _dg(c_bf16, h1, dims) + (_dg(c_bf16, h2, dims) + _dg(c_bf16, h3, dims))


def _dotxc(x, c_bf16, dims=NN):
    h1, h2, h3 = _split3(x)
    return _dg(h1, c_bf16, dims) + (_dg(h2, c_bf16, dims) + _dg(h3, c_bf16, dims))


def _iota(shape, axis):
    return lax.broadcasted_iota(jnp.int32, shape, axis)


def _order_masks(n, rev):
    row = _iota((n, n), 0)
    col = _iota((n, n), 1)
    if rev:
        return col >= row, col > row
    return col <= row, col < row


def _pair_rows(x, first):
    return jnp.concatenate([jnp.where(first, x, 0.0), jnp.where(first, 0.0, x)], axis=0)


def _tri_inverse_pairs(a_list):
    n = a_list[0].shape[0]
    row = _iota((n, LANES), 0)
    col = _iota((n, LANES), 1) % n
    r2 = _iota((LANES, LANES), 0)
    c2 = _iota((LANES, LANES), 1)
    same_head = (r2 < n) == (c2 < n)

    def bdiag(m):
        return jnp.where(same_head, jnp.concatenate([m, m], axis=0), 0.0)

    eye = (row == col).astype(F32)
    t = [eye - jnp.where((row // 2) == (col // 2), a, 0.0) for a in a_list]
    b = 2
    while b < n:
        off = jnp.logical_and((row // (2 * b)) == (col // (2 * b)), (row // b) != (col // b))
        x = [_dot1(jnp.where(off, a, 0.0), bdiag(ti)) for a, ti in zip(a_list, t)]
        t = [ti - _dot1(ti, bdiag(xi)) for ti, xi in zip(t, x)]
        b *= 2
    return t


def _sigmoid(x):
    return 1.0 / (1.0 + jnp.exp(-x))


def _softplus(x):
    return jnp.maximum(x, 0.0) + jnp.log(1.0 + jnp.exp(-jnp.abs(x)))


def _silu(x):
    return x * _sigmoid(x)


def _params(sem):
    return pltpu.CompilerParams(dimension_semantics=sem, vmem_limit_bytes=VMEM_LIMIT_BYTES)


def _chunk_maps(n, n_ctx, cps):
    n_blocks = n // (CHUNK * cps)
    ctx_blocks = n_ctx // (CHUNK * cps)

    def fwd(i):
        return i

    def bwd(i):
        return jnp.where(i < ctx_blocks, ctx_blocks - 1 - i, n_blocks + ctx_blocks - 1 - i)

    return n_blocks, fwd, bwd


def _mod_kernel(cc_ref, w_ref, b_ref, o_ref):
    s = _silu(cc_ref[...])
    o_ref[...] = _dot3(s, w_ref[...]) + b_ref[...]


def _modulation(cc, w_mod, b_mod):
    depth = w_mod.shape[0]
    ncol = w_mod.shape[2] // D_MODEL
    return pl.pallas_call(
        _mod_kernel,
        grid=(depth, ncol),
        in_specs=[
            pl.BlockSpec((SUBLANES, D_MODEL), lambda l, j: (0, 0)),
            pl.BlockSpec((None, D_MODEL, D_MODEL), lambda l, j: (l, 0, j)),
            pl.BlockSpec((None, 1, D_MODEL), lambda l, j: (l, 0, j)),
        ],
        out_specs=pl.BlockSpec((None, SUBLANES, D_MODEL), lambda l, j: (l, 0, j)),
        out_shape=jax.ShapeDtypeStruct((depth, SUBLANES, w_mod.shape[2]), F32),
        compiler_params=_params(("parallel", "parallel")),
        name="modulation",
    )(cc, w_mod, b_mod.reshape(depth, 1, -1))


def _seg_vec(mod_ref, idx, row0, nrows, n_ctx):
    lo = idx * D_MODEL
    ctx = mod_ref[0:1, lo:lo + D_MODEL]
    lat = mod_ref[1:2, lo:lo + D_MODEL]
    is_ctx = (_iota((nrows, 1), 0) + row0) < n_ctx
    return jnp.where(is_ctx, ctx, lat)


def _rmsnorm_rows(x, g):
    ms = jnp.mean(x * x, axis=-1, keepdims=True)
    return x * lax.rsqrt(ms + NORM_EPS) * g


def _norm_kernel(x_ref, g_ref, mod_ref, o_ref, *, n_ctx, tb, shift_idx, scale_idx):
    row0 = pl.program_id(0) * tb
    xn = _rmsnorm_rows(x_ref[...], g_ref[...])
    if scale_idx is not None:
        xn = xn * (1.0 + _seg_vec(mod_ref, scale_idx, row0, tb, n_ctx)) \
            + _seg_vec(mod_ref, shift_idx, row0, tb, n_ctx)
    o_ref[...] = xn.astype(o_ref.dtype)


def _norm(x, g, mod, n_ctx, tb, out_dtype, shift_idx=0, scale_idx=1):
    n = x.shape[0]
    return pl.pallas_call(
        functools.partial(_norm_kernel, n_ctx=n_ctx, tb=tb, shift_idx=shift_idx, scale_idx=scale_idx),
        grid=(n // tb,),
        in_specs=[
            pl.BlockSpec((tb, D_MODEL), lambda i: (i, 0)),
            pl.BlockSpec((1, D_MODEL), lambda i: (0, 0)),
            pl.BlockSpec(mod.shape, lambda i: (0, 0)),
        ],
        out_specs=pl.BlockSpec((tb, D_MODEL), lambda i: (i, 0)),
        out_shape=jax.ShapeDtypeStruct((n, D_MODEL), out_dtype),
        compiler_params=_params(("parallel",)),
        name="adaln_norm",
    )(x, g, mod)


def _mm_kernel(a_ref, w_ref, o_ref):
    o_ref[...] = _dg(a_ref[...], w_ref[...])


def _project(a, w, tm, name):
    n, k = a.shape
    cols = w.shape[1]
    return pl.pallas_call(
        _mm_kernel,
        grid=(n // tm,),
        in_specs=[pl.BlockSpec((tm, k), lambda i: (i, 0)),
                  pl.BlockSpec((k, cols), lambda i: (0, 0))],
        out_specs=pl.BlockSpec((tm, cols), lambda i: (i, 0)),
        out_shape=jax.ShapeDtypeStruct((n, cols), F32),
        compiler_params=_params(("parallel",)),
        name=name,
    )(a, w)


def _fill_ext(ext_ref, cur_ref, prev_ref, next_ref, has_prev, has_next, tb):
    zeros = jnp.zeros(prev_ref.shape, F32)
    ext_ref[0:SUBLANES, :] = jnp.where(has_prev, prev_ref[...], zeros)
    ext_ref[SUBLANES:SUBLANES + tb, :] = cur_ref[...]
    ext_ref[SUBLANES + tb:SUBLANES + tb + SUBLANES, :] = jnp.where(has_next, next_ref[...], zeros)


def _halo_maps(tb, nblk):
    r = tb // SUBLANES

    def prev_map(i):
        return (jnp.maximum(i * r - 1, 0), 0)

    def next_map(i):
        return (jnp.minimum((i + 1) * r, nblk * r - 1), 0)

    return prev_map, next_map


def _rw_prep_kernel(cur_ref, prev_ref, next_ref, mu_ref, w0_ref, w2_ref, a0_ref, a2_ref, g2_ref,
                    kka_ref, bd_ref,
                    r_ref, v_ref, kk_ref, lw_ref, kd_ref, bdir_ref, g_ref, bonus_ref,
                    *, tb, ctx_blocks, nblk):
    i = pl.program_id(0)
    seg_first = jnp.logical_or(i == 0, i == ctx_blocks)
    seg_last = jnp.logical_or(i == ctx_blocks - 1, i == nblk - 1)
    x = cur_ref[...]
    row = _iota((tb, 1), 0)
    prev_row = jnp.where(seg_first, 0.0, prev_ref[SUBLANES - 1:SUBLANES, :])
    next_row = jnp.where(seg_last, 0.0, next_ref[0:1, :])
    x_prev = jnp.where(row == 0, prev_row, pltpu.roll(x, 1, axis=0))
    x_next = jnp.where(row == tb - 1, next_row, pltpu.roll(x, tb - 1, axis=0))
    xs = mu_ref[0:1, :] * x_prev + mu_ref[1:2, :] * x + mu_ref[2:3, :] * x_next
    bw = BRANCH_W
    r = xs[:, 0:bw]
    k = xs[:, bw:2 * bw]
    v = xs[:, 2 * bw:3 * bw]
    wl = xs[:, 3 * bw:3 * bw + LANES]
    al = xs[:, 3 * bw + LANES:3 * bw + 2 * LANES]
    gl = xs[:, 3 * bw + 2 * LANES:3 * bw + 3 * LANES]
    bd = bd_ref[...]
    k_k = kka_ref[0:1, :]
    k_a = kka_ref[1:2, :]
    r_k = kka_ref[2:3, :]

    kk0 = k * k_k
    kk = kk0 * lax.rsqrt(_dotxc(kk0 * kk0, bd) + 1e-12)
    g = _dot3(_sigmoid(gl), g2_ref[...])
    twl = jnp.tanh(wl)
    ksum = jnp.zeros_like(k)
    for z in range(2):
        wlog = w0_ref[z:z + 1, :] + _dot3(twl, w2_ref[z])
        w = -_softplus(-wlog) - 0.5
        lw = -jnp.exp(w)
        a = _sigmoid(a0_ref[z:z + 1, :] + _dot3(al, a2_ref[z]))
        kd = k * (1.0 + (a - 1.0) * k_a)
        bdir = kk * a
        ksum = ksum + kd
        for p in range(RW_PAIRS):
            sl = slice(p * LANES, (p + 1) * LANES)
            lw_ref[z, p] = lw[:, sl]
            kd_ref[z, p] = kd[:, sl]
            bdir_ref[z, p] = bdir[:, sl]
    bonus = _dotxc(r * ksum * r_k, bd) * v
    for p in range(RW_PAIRS):
        sl = slice(p * LANES, (p + 1) * LANES)
        r_ref[p] = r[:, sl]
        v_ref[p] = v[:, sl]
        kk_ref[p] = kk[:, sl]
        g_ref[p] = g[:, sl]
        bonus_ref[p] = bonus[:, sl]


def _rw_prep(p_rw, wts, n_ctx, tb):
    n = p_rw.shape[0]
    nblk = n // tb
    prev_map, next_map = _halo_maps(tb, nblk)
    full = lambda a: pl.BlockSpec(a.shape, lambda i: (0,) * a.ndim)
    pair_spec = pl.BlockSpec((RW_PAIRS, tb, LANES), lambda i: (0, i, 0))
    dir_spec = pl.BlockSpec((2, RW_PAIRS, tb, LANES), lambda i: (0, 0, i, 0))
    pair_shape = jax.ShapeDtypeStruct((RW_PAIRS, n, LANES), F32)
    dir_shape = jax.ShapeDtypeStruct((2, RW_PAIRS, n, LANES), F32)
    params = [wts["mu3"], wts["w0"], wts["w2p"], wts["a0"], wts["a2p"], wts["g2"], wts["kka"], wts["bd64"]]
    return pl.pallas_call(
        functools.partial(_rw_prep_kernel, tb=tb, ctx_blocks=n_ctx // tb, nblk=nblk),
        grid=(nblk,),
        in_specs=[pl.BlockSpec((tb, RW_COLS), lambda i: (i, 0)),
                  pl.BlockSpec((SUBLANES, RW_COLS), prev_map),
                  pl.BlockSpec((SUBLANES, RW_COLS), next_map)]
                 + [full(a) for a in params],
        out_specs=[pair_spec, pair_spec, pair_spec, dir_spec, dir_spec, dir_spec, pair_spec, pair_spec],
        out_shape=[pair_shape, pair_shape, pair_shape, dir_shape, dir_shape, dir_shape, pair_shape, pair_shape],
        compiler_params=_params(("parallel",)),
        name="rwkv_prep",
    )(p_rw, p_rw, p_rw, *params)


def _rw_chunk_group(items, m_ref):
    n = CHUNK
    prow = _iota((n, LANES), 0)
    pcol = _iota((n, LANES), 1) % n
    masks = {False: (pcol <= prow, pcol < prow), True: (pcol >= prow, pcol > prow)}
    head_a = _iota((n, LANES), 1) < RW_HEAD
    head_a2 = jnp.concatenate([head_a, head_a], axis=1)
    row = _iota((LANES, LANES), 0)
    col = _iota((LANES, LANES), 1)
    same_head = (row < RW_HEAD) == (col < RW_HEAD)

    for d in items:
        r, v, kk, lw, kd, bdir = d["tile"]
        incl, _ = _order_masks(n, d["rev"])
        cum = _dotc(incl.astype(BF16), lw)
        cum_end = cum[0:1, :] if d["rev"] else cum[n - 1:n, :]
        e_inv = jnp.exp(-cum)
        e_end = jnp.exp(cum_end - cum)
        d.update(v=v, kkd=kk * jnp.exp(cum - lw), kinv=kd * e_inv, binv=bdir * e_inv, rd=r * jnp.exp(cum),
                 kend=kd * e_end, bend=bdir * e_end, g_end=jnp.exp(cum_end))

    for d in items:
        incl2, strict2 = masks[d["rev"]]
        lhs = jnp.concatenate([d["kkd"], d["rd"]], axis=0)
        with_b = _dot1(lhs, _pair_rows(d["binv"], head_a), NT)
        with_k = _dot1(lhs, _pair_rows(d["kinv"], head_a), NT)
        d["a_kb"] = jnp.where(strict2, with_b[:n], 0.0)
        d["a_rb"] = jnp.where(incl2, with_b[n:], 0.0)
        d["a_kk"] = jnp.where(strict2, with_k[:n], 0.0)
        d["a_rk"] = jnp.where(incl2, with_k[n:], 0.0)
        d["v_rows"] = _pair_rows(d["v"], head_a).astype(BF16)
    t = _tri_inverse_pairs([d["a_kb"] for d in items])
    for d, ti in zip(items, t):
        akv = _dot1(d["a_kk"], d["v_rows"])
        d["uw"] = _dot1(ti, _pair_rows(jnp.concatenate([akv, d["kkd"]], axis=1), head_a2))
    for d in items:
        rb_uw = _dot1(d["a_rb"], _pair_rows(d["uw"], head_a2))
        d["y0"] = _dot1(d["a_rk"], d["v_rows"]) - rb_uw[:, :LANES]
        d["qp"] = d["rd"] - rb_uw[:, LANES:]
        b_uw = _dot1(d["bend"], d["uw"], TN)
        kv = _dot1(d["kend"], d["v"], TN)
        bw_p = jnp.where(head_a, b_uw[:n, LANES:], b_uw[n:, LANES:])
        d["p_mat"] = jnp.where(prow == pcol, jnp.broadcast_to(d["g_end"], (n, LANES)), 0.0) - bw_p
        d["z0"] = jnp.where(head_a, kv[:n] - b_uw[:n, :LANES], kv[n:] - b_uw[n:, :LANES])

    for z, p in sorted({(d["z"], d["p"]) for d in items}):
        mine = sorted((d for d in items if (d["z"], d["p"]) == (z, p)), key=lambda d: d["c"], reverse=bool(z))
        m = m_ref[z, p]
        for d in mine:
            m_bd = jnp.where(same_head, jnp.concatenate([m, m], axis=0), 0.0)
            qm_pm = _dot3(jnp.concatenate([d["qp"], d["p_mat"]], axis=0), m_bd)
            d["y_ref"][p, d["rows"], :] = d["y0"] + qm_pm[:n]
            m = qm_pm[n:] + d["z0"]
        m_ref[z, p] = m


def _rw_chunk_kernel(rf, vf, kkf, lwf, kdf, bf, rb, vb, kkb, lwb, kdb, bb, yf_ref, yb_ref, m_ref):
    @pl.when(pl.program_id(0) == 0)
    def _():
        m_ref[...] = jnp.zeros(m_ref.shape, F32)

    dirs = ((rf, vf, kkf, lwf, kdf, bf, yf_ref, False), (rb, vb, kkb, lwb, kdb, bb, yb_ref, True))
    items = []
    for z, (r, v, kk, lw, kd, b, y_ref, rev) in enumerate(dirs):
        for p in range(RW_PAIRS):
            for c in range(RW_CPS):
                rows = slice(c * CHUNK, (c + 1) * CHUNK)
                items.append(dict(z=z, p=p, c=c, rev=rev, rows=rows, y_ref=y_ref,
                                  tile=tuple(x[p, rows, :] for x in (r, v, kk, lw, kd, b))))
    _rw_chunk_group(items, m_ref)


def _rw_chunk(r, v, kk, lw, kd, bdir, n_ctx):
    n = r.shape[1]
    n_blocks, fwd, bwd = _chunk_maps(n, n_ctx, RW_CPS)
    rows = CHUNK * RW_CPS

    def shared(cm):
        return pl.BlockSpec((RW_PAIRS, rows, LANES), lambda i: (0, cm(i), 0))

    def per_dir(z, cm):
        return pl.BlockSpec((None, RW_PAIRS, rows, LANES), lambda i: (z, 0, cm(i), 0))

    in_specs = [shared(fwd), shared(fwd), shared(fwd), per_dir(0, fwd), per_dir(0, fwd), per_dir(0, fwd),
                shared(bwd), shared(bwd), shared(bwd), per_dir(1, bwd), per_dir(1, bwd), per_dir(1, bwd)]
    shape = jax.ShapeDtypeStruct((RW_PAIRS, n, LANES), F32)
    return pl.pallas_call(
        _rw_chunk_kernel,
        grid=(n_blocks,),
        in_specs=in_specs,
        out_specs=[shared(fwd), shared(bwd)],
        out_shape=[shape, shape],
        scratch_shapes=[pltpu.VMEM((2, RW_PAIRS, CHUNK, LANES), F32)],
        compiler_params=_params(("arbitrary",)),
        name="rwkv_chunk",
    )(r, v, kk, lw, kd, bdir, r, v, kk, lw, kd, bdir)


def _gla_chunk_kernel(qf, kf, vf, alf, qb, kb, vb, alb, a2_ref, ab_ref, of_ref, ob_ref, s_ref):
    @pl.when(pl.program_id(0) == 0)
    def _():
        s_ref[...] = jnp.zeros(s_ref.shape, F32)

    n = CHUNK
    head_a = _iota((n, LANES), 1) < GLA_DK
    top = _iota((LANES, LANES), 0) < GLA_DK
    items = []
    for z, (q_ref, k_ref, v_ref, al_ref, o_ref) in enumerate(((qf, kf, vf, alf, of_ref), (qb, kb, vb, alb, ob_ref))):
        rev = bool(z)
        incl, _ = _order_masks(n, rev)
        for c in range(GLA_CPS):
            rows = slice(c * n, (c + 1) * n)
            zlog = _dot3(al_ref[rows, :], a2_ref[z]) + ab_ref[z:z + 1, :]
            log_a = (jnp.minimum(zlog, 0.0) - jnp.log(1.0 + jnp.exp(-jnp.abs(zlog)))) * (1.0 / GLA_GATE_NORM)
            bcum = _dotc(incl.astype(BF16), log_a)
            b_end = bcum[0:1, :] if rev else bcum[n - 1:n, :]
            q_dec = q_ref[rows, :] * (GLA_DK ** -0.5) * jnp.exp(bcum)
            k_inv = k_ref[rows, :] * jnp.exp(-bcum)
            k_end = k_ref[rows, :] * jnp.exp(b_end - bcum)
            dec_end = jnp.exp(b_end)
            for p in range(GLA_HEADS // 2):
                sl = slice(p * LANES, (p + 1) * LANES)
                items.append(dict(z=z, c=c, p=p, incl=incl, rows=rows, o_ref=o_ref, qd=q_dec[:, sl], ki=k_inv[:, sl],
                                  ke=k_end[:, sl], dec=dec_end[:, sl],
                                  v=[v_ref[rows, (2 * p + h) * GLA_DV:(2 * p + h + 1) * GLA_DV] for h in range(2)]))
    for d in items:
        d["qd_h"] = [jnp.where(head_a if h == 0 else jnp.logical_not(head_a), d["qd"], 0.0) for h in range(2)]
        d["att"] = [jnp.where(d["incl"], _dot1(d["qd_h"][h], d["ki"], NT), 0.0) for h in range(2)]
    for d in items:
        d["intra"] = [_dot1(d["att"][h], d["v"][h]) for h in range(2)]
        upd = [_dot1(d["ke"], d["v"][h], TN) for h in range(2)]
        d["upd"] = jnp.where(top, upd[0], upd[1])
        d["dec_col"] = jnp.transpose(jnp.broadcast_to(d["dec"], (LANES, LANES)))
    for z in range(2):
        order = range(GLA_CPS - 1, -1, -1) if z else range(GLA_CPS)
        for p in range(GLA_HEADS // 2):
            s = s_ref[z, p]
            for c in order:
                d = next(x for x in items if (x["z"], x["c"], x["p"]) == (z, c, p))
                inter = _dot3(jnp.concatenate(d["qd_h"], axis=0), s)
                for h in range(2):
                    hh = 2 * p + h
                    d["o_ref"][d["rows"], hh * GLA_DV:(hh + 1) * GLA_DV] = d["intra"][h] + inter[h * n:(h + 1) * n]
                s = d["dec_col"] * s + d["upd"]
            s_ref[z, p] = s


def _gla_chunk(p_gla, a2p, ab, n_ctx):
    n = p_gla.shape[0]
    n_blocks, fwd, bwd = _chunk_maps(n, n_ctx, GLA_CPS)
    hk = GLA_HEADS * GLA_DK
    rows = CHUNK * GLA_CPS

    def specs(cm):
        return [pl.BlockSpec((rows, hk), lambda i: (cm(i), 0)),
                pl.BlockSpec((rows, hk), lambda i: (cm(i), 1)),
                pl.BlockSpec((rows, BRANCH_W), lambda i: (cm(i), 1)),
                pl.BlockSpec((rows, LANES), lambda i: (cm(i), (2 * hk + 2 * BRANCH_W) // LANES))]

    full = lambda a: pl.BlockSpec(a.shape, lambda i: (0,) * a.ndim)
    shape = jax.ShapeDtypeStruct((n, BRANCH_W), F32)
    return pl.pallas_call(
        _gla_chunk_kernel,
        grid=(n_blocks,),
        in_specs=specs(fwd) + specs(bwd) + [full(a2p), full(ab)],
        out_specs=[pl.BlockSpec((rows, BRANCH_W), lambda i: (fwd(i), 0)),
                   pl.BlockSpec((rows, BRANCH_W), lambda i: (bwd(i), 0))],
        out_shape=[shape, shape],
        scratch_shapes=[pltpu.VMEM((2, GLA_HEADS // 2, LANES, LANES), F32)],
        compiler_params=_params(("arbitrary",)),
        name="gla_chunk",
    )(p_gla, p_gla, p_gla, p_gla, p_gla, p_gla, p_gla, p_gla, a2p, ab)


def _gated_norm_kernel(of_ref, ob_ref, gate_ref, g_ref, o_ref):
    for h in range(BRANCH_W // LANES):
        sl = slice(h * LANES, (h + 1) * LANES)
        o = of_ref[:, sl] + ob_ref[:, sl]
        ms = jnp.mean(o * o, axis=-1, keepdims=True)
        o_ref[:, sl] = (o * lax.rsqrt(ms + NORM_EPS) * g_ref[...] * _silu(gate_ref[:, sl])).astype(o_ref.dtype)


def _gated_norm(of, ob, gate_src, gate_block, norm_g, tb, name):
    n = of.shape[0]
    spec = pl.BlockSpec((tb, BRANCH_W), lambda i: (i, 0))
    return pl.pallas_call(
        _gated_norm_kernel,
        grid=(n // tb,),
        in_specs=[spec, spec, pl.BlockSpec((tb, BRANCH_W), lambda i: (i, gate_block)),
                  pl.BlockSpec((1, LANES), lambda i: (0, 0))],
        out_specs=spec,
        out_shape=jax.ShapeDtypeStruct((n, BRANCH_W), BF16),
        compiler_params=_params(("parallel",)),
        name=name,
    )(of, ob, gate_src, norm_g)


def _gdn_prep_kernel(cur_ref, prev_ref, next_ref, sc_ref, cw_ref, gp_ref,
                     q_ref, k_ref, v_ref, gb_ref, ext_ref, *, tb, ctx_blocks, nblk):
    i = pl.program_id(0)
    seg_first = jnp.logical_or(i == 0, i == ctx_blocks)
    seg_last = jnp.logical_or(i == ctx_blocks - 1, i == nblk - 1)
    _fill_ext(ext_ref, cur_ref, prev_ref, next_ref, jnp.logical_not(seg_first),
              jnp.logical_not(seg_last), tb)
    pad = GDN_CONV // 2
    acc = cw_ref[0:1, :] * ext_ref[pl.ds(SUBLANES - pad, tb), :]
    for t in range(1, GDN_CONV):
        acc = acc + cw_ref[t:t + 1, :] * ext_ref[pl.ds(SUBLANES - pad + t, tb), :]
    qkv = _silu(acc)
    for h in range(GDN_HEADS):
        for j, ref in enumerate((q_ref, k_ref, v_ref)):
            sl = slice(j * BRANCH_W + h * GDN_HEAD, j * BRANCH_W + (h + 1) * GDN_HEAD)
            osl = slice(h * GDN_HEAD, (h + 1) * GDN_HEAD)
            x = qkv[:, sl]
            if j < 2:
                x = x * lax.rsqrt(jnp.sum(x * x, axis=-1, keepdims=True) + 1e-12)
            if j == 0:
                x = x * (GDN_HEAD ** -0.5)
            ref[:, osl] = x
    sc = sc_ref[...]
    lane = _iota(sc.shape, 1)
    nh2 = 2 * GDN_HEADS
    g = -jnp.exp(gp_ref[0:1, :]) * _softplus(sc + gp_ref[1:2, :])
    beta = _sigmoid(sc)
    gb_ref[...] = jnp.where(lane < nh2, g, jnp.where(lane < 2 * nh2, beta, 0.0))


def _gdn_prep(p_gdn, conv_w, gparams, n_ctx, tb):
    n = p_gdn.shape[0]
    nblk = n // tb
    qkv_cols = 3 * BRANCH_W
    prev_map, next_map = _halo_maps(tb, nblk)
    sc_block = (4 * BRANCH_W) // LANES
    full = lambda a: pl.BlockSpec(a.shape, lambda i: (0,) * a.ndim)
    shape = jax.ShapeDtypeStruct((n, BRANCH_W), F32)
    spec = pl.BlockSpec((tb, BRANCH_W), lambda i: (i, 0))
    return pl.pallas_call(
        functools.partial(_gdn_prep_kernel, tb=tb, ctx_blocks=n_ctx // tb, nblk=nblk),
        grid=(nblk,),
        in_specs=[pl.BlockSpec((tb, qkv_cols), lambda i: (i, 0)),
                  pl.BlockSpec((SUBLANES, qkv_cols), prev_map),
                  pl.BlockSpec((SUBLANES, qkv_cols), next_map),
                  pl.BlockSpec((tb, LANES), lambda i: (i, sc_block)),
                  full(conv_w), full(gparams)],
        out_specs=[spec, spec, spec, pl.BlockSpec((tb, LANES), lambda i: (i, 0))],
        out_shape=[shape, shape, shape, jax.ShapeDtypeStruct((n, LANES), F32)],
        scratch_shapes=[pltpu.VMEM((tb + 2 * SUBLANES, qkv_cols), F32)],
        compiler_params=_params(("parallel",)),
        name="gdn_prep",
    )(p_gdn, p_gdn, p_gdn, p_gdn, conv_w, gparams)


def _gdn_chunk_kernel(qf, kf, vf, gbf, qb, kb, vb, gbb, of_ref, ob_ref, s_ref):
    @pl.when(pl.program_id(0) == 0)
    def _():
        s_ref[...] = jnp.zeros(s_ref.shape, F32)

    n = CHUNK
    w2 = 2 * GDN_HEAD
    lane = _iota((n, LANES), 1)
    head_a = lane < n
    first2 = _iota((n, w2), 1) < GDN_HEAD
    prow = _iota((n, LANES), 0)
    pcol = lane % n
    diag = _iota((LANES, LANES), 0) == _iota((LANES, LANES), 1)
    ones = jnp.ones((n, LANES), BF16)
    blocks = []
    for z, (q_ref, k_ref, v_ref, gb_ref, o_ref) in enumerate(((qf, kf, vf, gbf, of_ref), (qb, kb, vb, gbb, ob_ref))):
        rev = bool(z)
        incl_sq, _ = _order_masks(n, rev)
        for c in range(GDN_CPS):
            rows = slice(c * n, (c + 1) * n)
            blocks.append(dict(z=z, c=c, rev=rev, rows=rows, refs=(q_ref, k_ref, v_ref, o_ref), gb=gb_ref[rows, :],
                               tri=incl_sq.astype(BF16)))
    for b in blocks:
        b["gam_all"] = _dotc(b["tri"], b["gb"])
    for b in blocks:
        b["gam_parts"] = _split3(b["gam_all"])
    items = []
    for b in blocks:
        z, rev, rows, gb, gam_all = b["z"], b["rev"], b["rows"], b["gb"], b["gam_all"]
        q_ref, k_ref, v_ref, o_ref = b["refs"]
        incl = (pcol >= prow) if rev else (pcol <= prow)
        strict = (pcol > prow) if rev else (pcol < prow)
        for hp in range(GDN_HEADS // 2):
            sl = slice(hp * w2, (hp + 1) * w2)
            js = [z * GDN_HEADS + 2 * hp + i for i in range(2)]
            gam_h = [jnp.broadcast_to(gam_all[:, j:j + 1], (n, LANES)) for j in js]
            beta_h = [jnp.broadcast_to(gb[:, 2 * GDN_HEADS + j:2 * GDN_HEADS + j + 1], (n, LANES)) for j in js]
            pick = [jnp.concatenate([jnp.where(lane == js[0], part, 0.0).astype(BF16),
                                     jnp.where(lane == js[1], part, 0.0).astype(BF16)], axis=0)
                    for part in b["gam_parts"]]
            gam2 = jnp.concatenate(gam_h, axis=1)
            items.append(dict(z=z, c=b["c"], hp=hp, rows=rows, o_ref=o_ref, strict=strict, incl=incl,
                              pick=pick, gam_col=jnp.where(head_a, gam_h[0], gam_h[1]),
                              q=q_ref[rows, sl], k=k_ref[rows, sl], v=v_ref[rows, sl], gam2=gam2,
                              gam_end2=gam2[0:1, :] if rev else gam2[n - 1:n, :],
                              beta2=jnp.concatenate(beta_h, axis=1),
                              beta_p=jnp.where(head_a, beta_h[0], beta_h[1])))
    for d in items:
        d["gam_row"] = sum(_dg(ones, pk, NT) for pk in d["pick"])
    for d in items:
        d["decay"] = jnp.where(d["incl"], jnp.exp(jnp.where(d["incl"], d["gam_col"] - d["gam_row"], 0.0)), 0.0)

    for d in items:
        d["e_gam2"] = jnp.exp(d["gam2"])
        kq = _dot1(jnp.concatenate([d["k"], d["q"]], axis=0), _pair_rows(d["k"], first2), NT)
        d["a"] = jnp.where(d["strict"], d["beta_p"] * kq[:n] * d["decay"], 0.0)
        d["qk"] = kq[n:] * d["decay"]
    t = _tri_inverse_pairs([d["a"] for d in items])

    def heads_of(m):
        return m[:, :n], pltpu.roll(m, n, axis=1)[:, :n]

    for d, ti in zip(items, t):
        bv = d["beta2"] * d["v"]
        bek = d["beta2"] * d["e_gam2"] * d["k"]
        d["uw"] = []
        for i, t_h in enumerate(heads_of(ti)):
            hs = slice(i * GDN_HEAD, (i + 1) * GDN_HEAD)
            d["uw"].append(_dot1(t_h, jnp.concatenate([bv[:, hs], bek[:, hs]], axis=1)))
    for d in items:
        k_end = d["k"] * jnp.exp(d["gam_end2"] - d["gam2"])
        dec_end = jnp.exp(d["gam_end2"])
        q_dec = d["q"] * d["e_gam2"]
        d["p"], d["z0"], d["y0"], d["qp"] = [], [], [], []
        for i, qk_h in enumerate(heads_of(d["qk"])):
            hs = slice(i * GDN_HEAD, (i + 1) * GDN_HEAD)
            qx = _dot1(qk_h, d["uw"][i])
            d["y0"].append(qx[:, :LANES])
            d["qp"].append(q_dec[:, hs] - qx[:, LANES:])
            ke_uw = _dot1(k_end[:, hs], d["uw"][i], TN)
            d["p"].append(jnp.where(diag, jnp.broadcast_to(dec_end[:, hs], (LANES, LANES)), 0.0) - ke_uw[:, LANES:])
            d["z0"].append(ke_uw[:, :LANES])
    for z in range(2):
        order = range(GDN_CPS - 1, -1, -1) if z else range(GDN_CPS)
        for hp in range(GDN_HEADS // 2):
            for i in range(2):
                h = 2 * hp + i
                s = s_ref[z, h]
                for c in order:
                    d = next(x for x in items if (x["z"], x["hp"], x["c"]) == (z, hp, c))
                    qs_ps = _dot3(jnp.concatenate([d["qp"][i], d["p"][i]], axis=0), s)
                    d["o_ref"][d["rows"], h * GDN_HEAD:(h + 1) * GDN_HEAD] = d["y0"][i] + qs_ps[:n]
                    s = qs_ps[n:] + d["z0"][i]
                s_ref[z, h] = s


def _gdn_chunk(q, k, v, gb, n_ctx):
    n = q.shape[0]
    n_blocks, fwd, bwd = _chunk_maps(n, n_ctx, GDN_CPS)
    rows = CHUNK * GDN_CPS

    def specs(cm):
        wide = pl.BlockSpec((rows, BRANCH_W), lambda i: (cm(i), 0))
        return [wide, wide, wide, pl.BlockSpec((rows, LANES), lambda i: (cm(i), 0))]

    shape = jax.ShapeDtypeStruct((n, BRANCH_W), F32)
    return pl.pallas_call(
        _gdn_chunk_kernel,
        grid=(n_blocks,),
        in_specs=specs(fwd) + specs(bwd),
        out_specs=[pl.BlockSpec((rows, BRANCH_W), lambda i: (fwd(i), 0)),
                   pl.BlockSpec((rows, BRANCH_W), lambda i: (bwd(i), 0))],
        out_shape=[shape, shape],
        scratch_shapes=[pltpu.VMEM((2, GDN_HEADS, GDN_HEAD, GDN_HEAD), F32)],
        compiler_params=_params(("arbitrary",)),
        name="gdn_chunk",
    )(q, k, v, gb, q, k, v, gb)


def _merge_kernel(hn_ref, yf_ref, yb_ref, bonus_ref, g_ref, ln_ref, bd_ref, ygla_ref, of_ref, ob_ref, zg_ref,
                  gng_ref, h_ref, wg_ref, wb_ref, wo_ref, mod_ref, o_ref, *, n_ctx, tb):
    hn = hn_ref[...]
    bd = bd_ref[...]
    inv_n = 1.0 / RW_HEAD
    pairs = [slice(p * LANES, (p + 1) * LANES) for p in range(RW_PAIRS)]
    heads = [slice(hd * GDN_HEAD, (hd + 1) * GDN_HEAD) for hd in range(GDN_HEADS)]

    def gate(g):
        return _sigmoid(_dg(hn, wg_ref[:, g * D_MODEL:(g + 1) * D_MODEL]))

    ys = [yf_ref[p] + yb_ref[p] for p in range(RW_PAIRS)]
    os_ = [of_ref[:, sl] + ob_ref[:, sl] for sl in heads]
    gate_rw = gate(0)
    mus = [_dotxc(y, bd) * inv_n for y in ys]
    ds = [y - mu for y, mu in zip(ys, mus)]
    ms = [jnp.mean(o * o, axis=-1, keepdims=True) for o in os_]
    gate_gla = gate(1)
    var = [_dotxc(d * d, bd) * inv_n for d in ds]
    y_gdn = [o * lax.rsqrt(m + NORM_EPS) * gng_ref[...] * _silu(zg_ref[:, sl]) for o, m, sl in zip(os_, ms, heads)]
    gate_gdn = gate(2)
    y_rw = [(d * lax.rsqrt(v + RW_GN_EPS) * ln_ref[0:1, sl] + ln_ref[1:2, sl] + bonus_ref[p]) * g_ref[p]
            for p, (d, v, sl) in enumerate(zip(ds, var, pairs))]
    rw = sum(_dg(y.astype(BF16), wb_ref[0, sl, :]) for y, sl in zip(y_rw, pairs))
    gdn = sum(_dg(y.astype(BF16), wb_ref[2, sl, :]) for y, sl in zip(y_gdn, heads))
    gla = _dg(ygla_ref[...], wb_ref[1])
    merged = rw * gate_rw + gla * gate_gla + gdn * gate_gdn
    mix = _dg(merged.astype(BF16), wo_ref[...])
    gate_vec = _seg_vec(mod_ref, 2, pl.program_id(0) * tb, tb, n_ctx)
    o_ref[...] = h_ref[...] + mix * gate_vec


def _merge(hn, yf, yb, bonus, g, ln, bd128, y_gla, of, ob, p_gdn, gdn_norm_g, h, wg, wb, wo, mod, n_ctx, tb):
    n = h.shape[0]
    full = lambda a: pl.BlockSpec(a.shape, lambda i: (0,) * a.ndim)
    wide = pl.BlockSpec((tb, D_MODEL), lambda i: (i, 0))
    half = pl.BlockSpec((tb, BRANCH_W), lambda i: (i, 0))
    pair = pl.BlockSpec((RW_PAIRS, tb, LANES), lambda i: (0, i, 0))
    zg = pl.BlockSpec((tb, BRANCH_W), lambda i: (i, 3))
    return pl.pallas_call(
        functools.partial(_merge_kernel, n_ctx=n_ctx, tb=tb),
        grid=(n // tb,),
        in_specs=[wide, pair, pair, pair, pair, full(ln), full(bd128), half, half, half, zg, full(gdn_norm_g),
                  wide, full(wg), full(wb), full(wo), full(mod)],
        out_specs=wide,
        out_shape=jax.ShapeDtypeStruct((n, D_MODEL), F32),
        compiler_params=_params(("parallel",)),
        name="merge_out",
    )(hn, yf, yb, bonus, g, ln, bd128, y_gla, of, ob, p_gdn, gdn_norm_g, h, wg, wb, wo, mod)


def _mlp_kernel(h_ref, g_ref, mod_ref, w1_ref, w2_ref, gn_ref, modn_ref, *rest, n_ctx, tb, nj, last):
    if last:
        o_ref, hn_ref, acc_ref = rest
    else:
        o_ref, hnext_ref, hn_ref, acc_ref = rest
    j = pl.program_id(1)
    row0 = pl.program_id(0) * tb

    @pl.when(j == 0)
    def _():
        xn = _rmsnorm_rows(h_ref[...], g_ref[...])
        xn = xn * (1.0 + _seg_vec(mod_ref, 4, row0, tb, n_ctx)) + _seg_vec(mod_ref, 3, row0, tb, n_ctx)
        hn_ref[...] = xn.astype(BF16)
        acc_ref[...] = jnp.zeros(acc_ref.shape, F32)

    hid = jnp.maximum(_dg(hn_ref[...], w1_ref[...]), 0.0)
    acc_ref[...] += _dg((hid * hid).astype(BF16), w2_ref[...])

    @pl.when(j == nj - 1)
    def _():
        h_new = h_ref[...] + acc_ref[...] * _seg_vec(mod_ref, 5, row0, tb, n_ctx)
        nxt = _rmsnorm_rows(h_new, gn_ref[...])
        if last:
            o_ref[...] = nxt
        else:
            o_ref[...] = h_new
            nxt = nxt * (1.0 + _seg_vec(modn_ref, 1, row0, tb, n_ctx)) + _seg_vec(modn_ref, 0, row0, tb, n_ctx)
            hnext_ref[...] = nxt.astype(BF16)


def _mlp(h, g, mod, w1, w2, g_next, mod_next, n_ctx, tb, hb, last):
    n = h.shape[0]
    nj = MLP_HIDDEN // hb
    wide = pl.BlockSpec((tb, D_MODEL), lambda i, j: (i, 0))
    vec = pl.BlockSpec((1, D_MODEL), lambda i, j: (0, 0))
    out_specs = [wide] if last else [wide, wide]
    out_shape = [jax.ShapeDtypeStruct((n, D_MODEL), F32)]
    if not last:
        out_shape.append(jax.ShapeDtypeStruct((n, D_MODEL), BF16))
    return pl.pallas_call(
        functools.partial(_mlp_kernel, n_ctx=n_ctx, tb=tb, nj=nj, last=last),
        grid=(n // tb, nj),
        in_specs=[wide, vec, pl.BlockSpec(mod.shape, lambda i, j: (0, 0)),
                  pl.BlockSpec((D_MODEL, hb), lambda i, j: (0, j)),
                  pl.BlockSpec((hb, D_MODEL), lambda i, j: (j, 0)),
                  vec, pl.BlockSpec(mod_next.shape, lambda i, j: (0, 0))],
        out_specs=out_specs,
        out_shape=out_shape,
        scratch_shapes=[pltpu.VMEM((tb, D_MODEL), BF16), pltpu.VMEM((tb, D_MODEL), F32)],
        compiler_params=_params(("parallel", "arbitrary")),
        name="mlp",
    )(h, g, mod, w1, w2, g_next, mod_next)


def _block_diag_ones(n, blk):
    r = jnp.arange(n) // blk
    return (r[:, None] == r[None, :]).astype(BF16)


def _pad_cols(w, cols):
    return jnp.pad(w, ((0, 0), (0, cols - w.shape[1])))


def _lora_pad(w2, lora):
    out = jnp.zeros((2, LANES, w2.shape[2]), F32)
    for z in range(2):
        out = out.at[z, z * lora:(z + 1) * lora].set(w2[z])
    return out


def _pick_block(n, n_ctx, candidates):
    for c in candidates:
        if n % c == 0 and n_ctx % c == 0:
            return c
    raise ValueError("token counts must be multiples of the chunk length")


def _pick_rows(n, candidates):
    for c in candidates:
        if n % c == 0:
            return c
    raise ValueError("unsupported token count")


def _to_colmajor(t, n_ctx, rows):
    lat = t[n_ctx:]
    lat = lat.reshape((rows, GRID_W) + lat.shape[1:]).swapaxes(0, 1).reshape(lat.shape)
    return jnp.concatenate([t[:n_ctx], lat], axis=0)


def _from_colmajor(t, n_ctx, rows):
    lat = t[n_ctx:]
    lat = lat.reshape((GRID_W, rows) + lat.shape[1:]).swapaxes(0, 1).reshape(lat.shape)
    return jnp.concatenate([t[:n_ctx], lat], axis=0)


def kernel(x, c, ctx, c_ctx, w_mod, b_mod, norm1_g, w_in, rw_mu, rw_w0, rw_w2, rw_a0, rw_a2, rw_g2, rw_kk, rw_ka, rw_rk, rw_ln_w, rw_ln_b, gla_a2, gla_ab, gla_norm_g, gdn_conv, gdn_a_log, gdn_dt_bias, gdn_norm_g, w_branch, w_out, norm2_g, w_mlp1, w_mlp2, final_g):
    bsz, n_lat, dm = x.shape
    assert bsz == 1 and dm == D_MODEL
    n_ctx = ctx.shape[1]
    n = n_ctx + n_lat
    rows = n_lat // GRID_W
    depth = w_in.shape[0]
    tb = _pick_block(n, n_ctx, (256, 128))
    tm = _pick_rows(n, (640, 512, 256, 128))
    tmlp = _pick_rows(n, (640, 512, 256, 128))

    cc = jnp.zeros((SUBLANES, D_MODEL), F32).at[0].set(c_ctx).at[1].set(c[0])
    mod_all = _modulation(cc, w_mod, b_mod)

    bd64 = _block_diag_ones(BRANCH_W, RW_HEAD)
    bd128 = _block_diag_ones(LANES, RW_HEAD)
    h = jnp.concatenate([ctx[0], x[0]], axis=0)
    w_in_bf = w_in.astype(BF16)

    for l in range(depth):
        mod = mod_all[l]
        o0 = 0
        w_l = w_in_bf[l]
        w_rw = w_l[:, o0:o0 + RW_COLS]
        o0 += RW_COLS
        w_gla = _pad_cols(w_l[:, o0:o0 + GLA_COLS], GLA_COLS_PAD)
        o0 += GLA_COLS
        w_gdn = _pad_cols(w_l[:, o0:o0 + GDN_COLS], GDN_COLS_PAD)
        o0 += GDN_COLS
        w_gate = w_l[:, o0:]

        if l == 0:
            hn = _norm(h, norm1_g[l][None], mod, n_ctx, tb, BF16)
        hn_cm = _to_colmajor(hn, n_ctx, rows)
        p_rw = _project(hn, w_rw, tm, "proj_rwkv")
        p_gla = _project(hn_cm, w_gla, tm, "proj_gla")
        p_gdn = _project(hn, w_gdn, tm, "proj_gdn")

        mu = rw_mu[l]
        rw_wts = {
            "mu3": jnp.zeros((SUBLANES, RW_COLS), F32).at[0].set(mu[0]).at[1].set(1 - mu[0] - mu[1]).at[2].set(mu[1]),
            "w0": rw_w0[l], "w2p": _lora_pad(rw_w2[l], RW_DECAY_LORA),
            "a0": rw_a0[l], "a2p": _lora_pad(rw_a2[l], RW_A_LORA), "g2": rw_g2[l],
            "kka": jnp.zeros((SUBLANES, BRANCH_W), F32).at[0].set(rw_kk[l]).at[1].set(rw_ka[l])
                      .at[2].set(rw_rk[l].reshape(-1)),
            "bd64": bd64,
        }
        r, v, kk, lw, kd, bdir, g, bonus = _rw_prep(p_rw, rw_wts, n_ctx, tb)
        yf, yb = _rw_chunk(r, v, kk, lw, kd, bdir, n_ctx)
        ln = jnp.zeros((SUBLANES, BRANCH_W), F32).at[0].set(rw_ln_w[l]).at[1].set(rw_ln_b[l])

        of_gla, ob_gla = _gla_chunk(p_gla, _lora_pad(gla_a2[l], GLA_GATE_LORA), gla_ab[l], n_ctx)
        y_gla_cm = _gated_norm(of_gla, ob_gla, p_gla, 2, gla_norm_g[l][None], tb, "gla_post")
        y_gla = _from_colmajor(y_gla_cm, n_ctx, rows)

        conv_w = jnp.zeros((SUBLANES, 3 * BRANCH_W), F32).at[:GDN_CONV].set(gdn_conv[l])
        nh2 = 2 * GDN_HEADS
        gparams = jnp.zeros((SUBLANES, LANES), F32).at[0, :nh2].set(gdn_a_log[l].reshape(-1)) \
            .at[1, :nh2].set(gdn_dt_bias[l].reshape(-1))
        q_d, k_d, v_d, gb = _gdn_prep(p_gdn, conv_w, gparams, n_ctx, tb)
        of, ob = _gdn_chunk(q_d, k_d, v_d, gb, n_ctx)

        h = _merge(hn, yf, yb, bonus, g, ln, bd128, y_gla, of, ob, p_gdn, gdn_norm_g[l][None], h, w_gate,
                   w_branch[l].astype(BF16), w_out[l].astype(BF16), mod, n_ctx, tb)
        last = l == depth - 1
        g_next = final_g[None] if last else norm1_g[l + 1][None]
        mod_next = mod if last else mod_all[l + 1]
        res = _mlp(h, norm2_g[l][None], mod, w_mlp1[l].astype(BF16), w_mlp2[l].astype(BF16), g_next, mod_next,
                   n_ctx, tmlp, MLP_HIDDEN // 2, last)
        if last:
            out = res[0]
        else:
            h, hn = res

    return out[n_ctx:][None]
```
